```python
import jax, jax.numpy as jnp
from jax import lax

D_MODEL = 2048
BATCH = 8
SEQ = 2048
DEPTH = 1

CHUNK = 64
N_META = 16
D_RNN = D_MODEL
LRU_BLOCKS = 16
LRU_BS = D_RNN // LRU_BLOCKS
CONV_W = 4
LRU_C = 8.0
N_HEADS = 16
HEAD_DIM = 128
D_ATTN = N_HEADS * HEAD_DIM
Q_BLOCK = 128
D_FF = 5632
N_NORMS = 6
EPS = 1e-6
SPLITS = (D_RNN, D_RNN, D_ATTN, D_ATTN, D_ATTN, N_HEADS, D_MODEL, D_MODEL)
N_IN = sum(SPLITS)

kernel_name = "hybrid_rglru_fox_macaron_block"


def rms_norm(x, g):
    xf = x.astype(jnp.float32)
    y = xf * lax.rsqrt(jnp.mean(xf * xf, axis=-1, keepdims=True) + EPS)
    return (y * g.astype(jnp.float32)).astype(x.dtype)


def swiglu(x, w_gu, w_down):
    gu = x @ w_gu
    gate, up = jnp.split(gu, 2, axis=-1)
    return (jax.nn.silu(gate) * up) @ w_down


def causal_depthwise_conv(x, w, b):
    T = x.shape[1]
    xp = jnp.pad(x, ((0, 0), (CONV_W - 1, 0), (0, 0)))
    y = xp[:, 0:T] * w[0]
    for k in range(1, CONV_W):
        y = y + xp[:, k:k + T] * w[k]
    return y + b


def _lru_combine(e1, e2):
    a1, b1 = e1
    a2, b2 = e2
    return a1 * a2, a2 * b1 + b2


def _scan_segment(a, b, h0):
    A, Bc = lax.associative_scan(_lru_combine, (a, b), axis=1)
    return A * h0[:, None, :] + Bc


def rg_lru(x, w_a, b_a, w_x, b_x, lam):
    B, T, _ = x.shape
    xb = x.reshape(B, T, LRU_BLOCKS, LRU_BS)
    r = jax.nn.sigmoid(jnp.einsum('btni,nij->btnj', xb, w_a).reshape(B, T, D_RNN) + b_a)
    i = jax.nn.sigmoid(jnp.einsum('btni,nij->btnj', xb, w_x).reshape(B, T, D_RNN) + b_x)
    log_a = (-LRU_C * jax.nn.softplus(-lam.astype(jnp.float32))) * r.astype(jnp.float32)
    a = jnp.exp(log_a)
    bvec = jnp.sqrt(-jnp.expm1(2.0 * log_a)) * (i * x).astype(jnp.float32)
    h_meta = _scan_segment(a[:, :N_META], bvec[:, :N_META], jnp.zeros((B, D_RNN), jnp.float32))
    n_chunks = (T - N_META) // CHUNK
    a_c = a[:, N_META:].reshape(B, n_chunks, CHUNK, D_RNN).transpose(1, 0, 2, 3)
    b_c = bvec[:, N_META:].reshape(B, n_chunks, CHUNK, D_RNN).transpose(1, 0, 2, 3)

    def step(h, ab):
        hc = _scan_segment(ab[0], ab[1], h)
        return hc[:, -1], hc

    _, hs = lax.scan(step, h_meta[:, -1], (a_c, b_c))
    h_real = hs.transpose(1, 0, 2, 3).reshape(B, T - N_META, D_RNN)
    return jnp.concatenate([h_meta, h_real], axis=1).astype(x.dtype)


def forgetting_attention(q, k, v, f_logit):
    B, T, _ = q.shape
    q = q.reshape(B, T, N_HEADS, HEAD_DIM).transpose(0, 2, 1, 3)
    k = k.reshape(B, T, N_HEADS, HEAD_DIM).transpose(0, 2, 1, 3)
    v = v.reshape(B, T, N_HEADS, HEAD_DIM).transpose(0, 2, 1, 3)
    log_f = jax.nn.log_sigmoid(f_logit.astype(jnp.float32))
    F = jnp.cumsum(log_f, axis=1).transpose(0, 2, 1)
    scale = HEAD_DIM ** -0.5
    pos = jnp.arange(T)
    outs = []
    for start in range(0, T, Q_BLOCK):
        end = min(start + Q_BLOCK, T)
        s = jnp.einsum('bhqd,bhkd->bhqk', q[:, :, start:end], k[:, :, :end]).astype(jnp.float32) * scale
        s = s + (F[:, :, start:end, None] - F[:, :, None, :end])
        mask = pos[start:end, None] >= pos[None, :end]
        s = jnp.where(mask, s, -jnp.inf)
        p = jax.nn.softmax(s, axis=-1).astype(v.dtype)
        outs.append(jnp.einsum('bhqk,bhkd->bhqd', p, v[:, :, :end]))
    o = jnp.concatenate(outs, axis=2)
    return o.transpose(0, 2, 1, 3).reshape(B, T, D_ATTN)


def hybrid_mixer(u, w_in, conv_w, conv_b, lru_w_a, lru_b_a, lru_w_x, lru_b_x, lru_lambda,
                 forget_b, w_rnn_out, w_attn_out, w_o):
    z = u @ w_in
    idx, acc = [], 0
    for s in SPLITS[:-1]:
        acc += s
        idx.append(acc)
    xr, gr, q, k, v, fl, g_rnn, g_attn = jnp.split(z, idx, axis=-1)
    xc = causal_depthwise_conv(xr, conv_w, conv_b)
    y_rnn = rg_lru(xc, lru_w_a, lru_b_a, lru_w_x, lru_b_x, lru_lambda) * jax.nn.gelu(gr)
    y_attn = forgetting_attention(q, k, v, fl + forget_b)
    merged = jax.nn.sigmoid(g_rnn) * (y_rnn @ w_rnn_out) + jax.nn.sigmoid(g_attn) * (y_attn @ w_attn_out)
    return merged @ w_o


def setup_inputs(seed: int = 0) -> dict:
    key = jax.random.key(seed)
    ks = jax.random.split(key, 20)
    f32 = jnp.float32
    nrm = lambda k, shp, sc: jax.random.normal(k, shp, f32) * sc
    a0 = jax.random.uniform(ks[13], (DEPTH, D_RNN), f32, 0.9, 0.999)
    s0 = a0 ** (1.0 / LRU_C)
    lam = jnp.log(s0) - jnp.log1p(-s0)
    return {
        "x": nrm(ks[0], (BATCH, SEQ, D_MODEL), 1.0),
        "meta_tokens": nrm(ks[1], (N_META, D_MODEL), 1.0),
        "norm_g": 1.0 + nrm(ks[2], (DEPTH, N_NORMS, D_MODEL), 0.05),
        "ffn1_w_gu": nrm(ks[3], (DEPTH, D_MODEL, 2 * D_FF), D_MODEL ** -0.5),
        "ffn1_w_down": nrm(ks[4], (DEPTH, D_FF, D_MODEL), D_FF ** -0.5),
        "w_in": nrm(ks[5], (DEPTH, D_MODEL, N_IN), D_MODEL ** -0.5),
        "conv_w": nrm(ks[6], (DEPTH, CONV_W, D_RNN), CONV_W ** -0.5),
        "conv_b": nrm(ks[7], (DEPTH, D_RNN), 0.01),
        "lru_w_a": nrm(ks[8], (DEPTH, LRU_BLOCKS, LRU_BS, LRU_BS), LRU_BS ** -0.5),
        "lru_b_a": nrm(ks[9], (DEPTH, D_RNN), 0.01),
        "lru_w_x": nrm(ks[10], (DEPTH, LRU_BLOCKS, LRU_BS, LRU_BS), LRU_BS ** -0.5),
        "lru_b_x": nrm(ks[11], (DEPTH, D_RNN), 0.01),
        "lru_lambda": lam,
        "forget_b": jax.random.uniform(ks[12], (DEPTH, N_HEADS), f32, 2.0, 6.0),
        "w_rnn_out": nrm(ks[14], (DEPTH, D_RNN, D_MODEL), D_RNN ** -0.5),
        "w_attn_out": nrm(ks[15], (DEPTH, D_ATTN, D_MODEL), D_ATTN ** -0.5),
        "w_o": nrm(ks[16], (DEPTH, D_MODEL, D_MODEL), D_MODEL ** -0.5),
        "ffn2_w_gu": nrm(ks[17], (DEPTH, D_MODEL, 2 * D_FF), D_MODEL ** -0.5),
        "ffn2_w_down": nrm(ks[18], (DEPTH, D_FF, D_MODEL), D_FF ** -0.5),
    }


def reference(x, meta_tokens, norm_g, ffn1_w_gu, ffn1_w_down, w_in, conv_w, conv_b,
              lru_w_a, lru_b_a, lru_w_x, lru_b_x, lru_lambda, forget_b,
              w_rnn_out, w_attn_out, w_o, ffn2_w_gu, ffn2_w_down):
    B = x.shape[0]
    meta = jnp.broadcast_to(meta_tokens.astype(x.dtype)[None], (B, N_META, D_MODEL))
    h = jnp.concatenate([meta, x], axis=1)
    for l in range(DEPTH):
        g = norm_g[l]
        u = rms_norm(h, g[0])
        h = h + 0.5 * rms_norm(swiglu(u, ffn1_w_gu[l], ffn1_w_down[l]), g[1])
        u = rms_norm(h, g[2])
        mix = hybrid_mixer(u, w_in[l], conv_w[l], conv_b[l], lru_w_a[l], lru_b_a[l],
                           lru_w_x[l], lru_b_x[l], lru_lambda[l], forget_b[l],
                           w_rnn_out[l], w_attn_out[l], w_o[l])
        h = h + rms_norm(mix, g[3])
        u = rms_norm(h, g[4])
        h = h + 0.5 * rms_norm(swiglu(u, ffn2_w_gu[l], ffn2_w_down[l]), g[5])
    return h[:, N_META:]
```

```python
import functools

import jax
import jax.numpy as jnp
from jax import lax
from jax.experimental import pallas as pl
from jax.experimental.pallas import tpu as pltpu

D_MODEL = 2048
N_META = 16
D_FF = 5632
N_HEADS = 16
HEAD_DIM = 128
LRU_BS = 128
LRU_BLOCKS = D_MODEL // LRU_BS
CONV_W = 4
LRU_C = 8.0
EPS = 1e-6
N_PIECES = 7
Z_COLS = N_PIECES * D_MODEL
(P_XR, P_GR, P_Q, P_K, P_V, P_GRNN, P_GATTN) = range(N_PIECES)

LANES = 128
SUBLANES = 8
VMEM_BYTES_V7X = 64 * 1024 * 1024
NEG_BIG = -1e30

F32 = jnp.float32
BF16 = jnp.bfloat16


def _vmem_limit(pipelined_bytes, resident_bytes):
    return min(2 * pipelined_bytes + resident_bytes + (4 << 20), VMEM_BYTES_V7X - (4 << 20))


def _rms(x, g):
    return (x * lax.rsqrt(jnp.mean(x * x, axis=-1, keepdims=True) + EPS)) * g


def _ffn_kernel(h_ref, gpre_ref, wg_ref, wu_ref, wd_ref, gpost_ref, gnext_ref,
                out_ref, unext_ref, u_scr, acc_scr):
    j = pl.program_id(1)

    @pl.when(j == 0)
    def _():
        u_scr[...] = _rms(h_ref[...], gpre_ref[...]).astype(BF16)
        acc_scr[...] = jnp.zeros_like(acc_scr)

    u = u_scr[...]
    gate = jnp.dot(u, wg_ref[...], preferred_element_type=F32)
    up = jnp.dot(u, wu_ref[...], preferred_element_type=F32)
    act = ((gate * jax.nn.sigmoid(gate)) * up).astype(BF16)
    acc_scr[...] += jnp.dot(act, wd_ref[...], preferred_element_type=F32)

    @pl.when(j == pl.num_programs(1) - 1)
    def _():
        h_new = h_ref[...] + 0.5 * _rms(acc_scr[...], gpost_ref[...])
        out_ref[...] = h_new
        unext_ref[...] = _rms(h_new, gnext_ref[...]).astype(BF16)


def _ffn(h, g_pre, w_gu, w_down, g_post, g_next, *, tm, tf):
    m = h.shape[0]
    nf = D_FF // tf
    blocks = tm * D_MODEL * (4 + 4 + 2) + 3 * D_MODEL * tf * 2
    resident = tm * D_MODEL * (2 + 4) + tm * tf * (4 + 4 + 2)
    return pl.pallas_call(
        _ffn_kernel,
        grid=(m // tm, nf),
        in_specs=[
            pl.BlockSpec((tm, D_MODEL), lambda i, j: (i, 0)),
            pl.BlockSpec((1, D_MODEL), lambda i, j: (0, 0)),
            pl.BlockSpec((D_MODEL, tf), lambda i, j: (0, j)),
            pl.BlockSpec((D_MODEL, tf), lambda i, j: (0, j + nf)),
            pl.BlockSpec((tf, D_MODEL), lambda i, j: (j, 0)),
            pl.BlockSpec((1, D_MODEL), lambda i, j: (0, 0)),
            pl.BlockSpec((1, D_MODEL), lambda i, j: (0, 0)),
        ],
        out_specs=[
            pl.BlockSpec((tm, D_MODEL), lambda i, j: (i, 0)),
            pl.BlockSpec((tm, D_MODEL), lambda i, j: (i, 0)),
        ],
        out_shape=[
            jax.ShapeDtypeStruct((m, D_MODEL), F32),
            jax.ShapeDtypeStruct((m, D_MODEL), BF16),
        ],
        scratch_shapes=[
            pltpu.VMEM((tm, D_MODEL), BF16),
            pltpu.VMEM((tm, D_MODEL), F32),
        ],
        compiler_params=pltpu.CompilerParams(
            dimension_semantics=("parallel", "arbitrary"),
            vmem_limit_bytes=_vmem_limit(blocks, resident)),
        name="ffn",
    )(h, g_pre, w_gu, w_gu, w_down, g_post, g_next)


def _in_proj_kernel(u_ref, w_ref, wfl_ref, z_ref, fl_ref):
    u = u_ref[...]
    z_ref[...] = jnp.dot(u, w_ref[...], preferred_element_type=F32).astype(BF16)

    @pl.when(pl.program_id(1) == 0)
    def _():
        fl_ref[...] = jnp.dot(u, wfl_ref[...], preferred_element_type=F32)


def _in_proj(u, w_main, w_fl, *, tm, tn):
    m = u.shape[0]
    blocks = tm * D_MODEL * 2 + D_MODEL * tn * 2 + D_MODEL * LANES * 2 + tm * tn * 2 + tm * LANES * 4
    resident = tm * tn * 4
    return pl.pallas_call(
        _in_proj_kernel,
        grid=(m // tm, Z_COLS // tn),
        in_specs=[
            pl.BlockSpec((tm, D_MODEL), lambda i, j: (i, 0)),
            pl.BlockSpec((D_MODEL, tn), lambda i, j: (0, j)),
            pl.BlockSpec((D_MODEL, LANES), lambda i, j: (0, 0)),
        ],
        out_specs=[
            pl.BlockSpec((tm, tn), lambda i, j: (i, j)),
            pl.BlockSpec((tm, LANES), lambda i, j: (i, 0)),
        ],
        out_shape=[
            jax.ShapeDtypeStruct((m, Z_COLS), BF16),
            jax.ShapeDtypeStruct((m, LANES), F32),
        ],
        compiler_params=pltpu.CompilerParams(
            dimension_semantics=("parallel", "arbitrary"),
            vmem_limit_bytes=_vmem_limit(blocks, resident)),
        name="in_proj",
    )(u, w_main, w_fl)


def _rnn_kernel(xr_ref, gr_ref, ctx_ref, h0_ref, cw_ref, cb_ref, wax_ref, ba_ref, bx_ref, lam_ref,
                y_ref, hlast_ref, xbuf, a_scr, b_scr, h_scr, carry_scr, *, tt, cb):
    t = pl.program_id(2)

    @pl.when(t == 0)
    def _():
        xbuf[0:SUBLANES, :] = ctx_ref[...]
        carry_scr[...] = h0_ref[...]

    x = xr_ref[0].astype(F32)
    xbuf[SUBLANES:SUBLANES + tt, :] = x
    xc = xbuf[pl.ds(SUBLANES - 3, tt), :] * cw_ref[0:1, :]
    xc = xc + xbuf[pl.ds(SUBLANES - 2, tt), :] * cw_ref[1:2, :]
    xc = xc + xbuf[pl.ds(SUBLANES - 1, tt), :] * cw_ref[2:3, :]
    xc = xc + x * cw_ref[3:4, :]
    xc = xc + cb_ref[...]
    xbuf[0:SUBLANES, :] = xbuf[tt:tt + SUBLANES, :]

    for n in range(cb // LRU_BS):
        cols = slice(n * LRU_BS, (n + 1) * LRU_BS)
        xn = xc[:, cols]
        g = jnp.dot(xn.astype(BF16), wax_ref[n], preferred_element_type=F32)
        r = jax.nn.sigmoid(g[:, :LRU_BS] + ba_ref[:, cols])
        i = jax.nn.sigmoid(g[:, LRU_BS:] + bx_ref[:, cols])
        neg_lam = -lam_ref[:, cols]
        softplus = jnp.maximum(neg_lam, 0.0) + jnp.log1p(jnp.exp(-jnp.abs(neg_lam)))
        log_a = (-LRU_C * softplus) * r
        a = jnp.exp(log_a)
        a_scr[:, cols] = a
        b_scr[:, cols] = jnp.sqrt(1.0 - a * a) * (i * xn)

    row = lax.broadcasted_iota(jnp.int32, (SUBLANES, cb), 0)

    def group(gi, h_prev):
        rows = pl.ds(pl.multiple_of(gi * SUBLANES, SUBLANES), SUBLANES)
        a8 = a_scr[rows, :]
        b8 = b_scr[rows, :]
        for d in (1, 2, 4):
            live = row >= d
            b8 = jnp.where(live, a8 * pltpu.roll(b8, d, axis=0) + b8, b8)
            a8 = jnp.where(live, a8 * pltpu.roll(a8, d, axis=0), a8)
        h8 = a8 * h_prev + b8
        h_scr[rows, :] = h8
        return h8[SUBLANES - 1:SUBLANES, :]

    h_last = lax.fori_loop(0, tt // SUBLANES, group, carry_scr[...], unroll=4)
    carry_scr[...] = h_last
    hlast_ref[0] = jnp.broadcast_to(h_last, (SUBLANES, cb))
    y_ref[0] = (h_scr[...] * jax.nn.gelu(gr_ref[0].astype(F32), approximate=True)).astype(BF16)


def _rnn(z3, ctx, h0, conv_w, conv_b, w_ax, b_a, b_x, lam, *, tt, cb):
    b, t_len, _ = z3.shape
    ncb = D_MODEL // cb
    nblk = cb // LRU_BS
    vec = lambda rows: pl.BlockSpec((rows, cb), lambda bi, ci, ti: (0, ci))
    blocks = 2 * tt * cb * 2 + tt * cb * 2 + nblk * LRU_BS * 2 * LRU_BS * 2
    resident = tt * cb * 4 * 8
    return pl.pallas_call(
        functools.partial(_rnn_kernel, tt=tt, cb=cb),
        grid=(b, ncb, t_len // tt),
        in_specs=[
            pl.BlockSpec((1, tt, cb), lambda bi, ci, ti: (bi, ti, P_XR * ncb + ci)),
            pl.BlockSpec((1, tt, cb), lambda bi, ci, ti: (bi, ti, P_GR * ncb + ci)),
            vec(SUBLANES), vec(1), vec(CONV_W), vec(1),
            pl.BlockSpec((nblk, LRU_BS, 2 * LRU_BS), lambda bi, ci, ti: (ci, 0, 0)),
            vec(1), vec(1), vec(1),
        ],
        out_specs=[
            pl.BlockSpec((1, tt, cb), lambda bi, ci, ti: (bi, ti, ci)),
            pl.BlockSpec((1, SUBLANES, cb), lambda bi, ci, ti: (bi, 0, ci)),
        ],
        out_shape=[
            jax.ShapeDtypeStruct((b, t_len, D_MODEL), BF16),
            jax.ShapeDtypeStruct((b, SUBLANES, D_MODEL), F32),
        ],
        scratch_shapes=[
            pltpu.VMEM((tt + SUBLANES, cb), F32),
            pltpu.VMEM((tt, cb), F32),
            pltpu.VMEM((tt, cb), F32),
            pltpu.VMEM((tt, cb), F32),
            pltpu.VMEM((1, cb), F32),
        ],
        compiler_params=pltpu.CompilerParams(
            dimension_semantics=("parallel", "parallel", "arbitrary"),
            vmem_limit_bytes=_vmem_limit(blocks, resident)),
        name="rnn",
    )(z3, z3, ctx, h0, conv_w, conv_b, w_ax, b_a, b_x, lam)


def _forget_kernel(x_ref, fb_ref, nf_ref, *, n_blocks, first_valid):
    lane = lax.broadcasted_iota(jnp.int32, (N_HEADS, LANES), 1)
    fb = fb_ref[...]
    run = jnp.zeros((N_HEADS, 1), F32)
    for k in range(n_blocks):
        x = x_ref[0, :, k * LANES:(k + 1) * LANES] + fb
        lf = jnp.minimum(x, 0.0) - jnp.log1p(jnp.exp(-jnp.abs(x)))
        if k == 0:
            lf = jnp.where(lane >= first_valid, lf, 0.0)
        d = 1
        while d < LANES:
            lf = jnp.where(lane >= d, lf + pltpu.roll(lf, d, axis=1), lf)
            d *= 2
        f = lf + run
        run = f[:, LANES - 1:LANES]
        neg = -f
        if k == 0:
            neg = jnp.where(lane >= first_valid, neg, NEG_BIG)
        nf_ref[0, :, k * LANES:(k + 1) * LANES] = neg


def _forget_prefix(x_all, forget_b):
    b, _, n = x_all.shape
    return pl.pallas_call(
        functools.partial(_forget_kernel, n_blocks=n // LANES, first_valid=LANES - N_META),
        grid=(b,),
        in_specs=[
            pl.BlockSpec((1, N_HEADS, n), lambda bi: (bi, 0, 0)),
            pl.BlockSpec((N_HEADS, 1), lambda bi: (0, 0)),
        ],
        out_specs=pl.BlockSpec((1, N_HEADS, n), lambda bi: (bi, 0, 0)),
        out_shape=jax.ShapeDtypeStruct((b, N_HEADS, n), F32),
        compiler_params=pltpu.CompilerParams(dimension_semantics=("parallel",)),
        name="forget_prefix",
    )(x_all, forget_b)


def _attn_kernel(q_ref, k_ref, v_ref, km_ref, vm_ref, nfm_ref, nf_ref, o_ref, *, tq):
    qi = pl.program_id(2)
    scale = HEAD_DIM ** -0.5
    q = q_ref[0]

    def scores(k_blk, bias):
        s = lax.dot_general(q, k_blk, (((1,), (1,)), ((), ())), preferred_element_type=F32)
        return s * scale + bias

    def update(carry, s, v_blk):
        m, l, acc = carry
        m_new = jnp.maximum(m, jnp.max(s, axis=1, keepdims=True))
        alpha = jnp.exp(m - m_new)
        p = jnp.exp(s - m_new)
        l = alpha * l + jnp.sum(p, axis=1, keepdims=True)
        acc = alpha * acc + jnp.dot(p.astype(BF16), v_blk, preferred_element_type=F32)
        return m_new, l, acc

    s = scores(km_ref[...], nfm_ref[0])
    m = jnp.max(s, axis=1, keepdims=True)
    p = jnp.exp(s - m)
    carry = (m, jnp.sum(p, axis=1, keepdims=True),
             jnp.dot(p.astype(BF16), vm_ref[...], preferred_element_type=F32))

    def full_block(j, carry):
        rows = pl.ds(pl.multiple_of(j * tq, tq), tq)
        return update(carry, scores(k_ref[0, rows, :], nf_ref[0, j]), v_ref[0, rows, :])

    carry = lax.fori_loop(0, qi, full_block, carry)

    rows = pl.ds(pl.multiple_of(qi * tq, tq), tq)
    s = scores(k_ref[0, rows, :], nf_ref[0, qi])
    causal = (lax.broadcasted_iota(jnp.int32, (tq, tq), 0)
              >= lax.broadcasted_iota(jnp.int32, (tq, tq), 1))
    _, l, acc = update(carry, jnp.where(causal, s, NEG_BIG), v_ref[0, rows, :])
    o_ref[0] = (acc / l).astype(BF16)


def _attention(z3, zm_pad, nf_meta, nf_real, *, tq):
    b, t_len, _ = z3.shape
    nq = t_len // tq
    blocks = tq * HEAD_DIM * 2 * 2 + 2 * t_len * HEAD_DIM * 2 + 2 * LANES * HEAD_DIM * 2
    resident = nq * SUBLANES * tq * 4 * 2 + 6 * tq * tq * 4
    return pl.pallas_call(
        functools.partial(_attn_kernel, tq=tq),
        grid=(b, N_HEADS, nq),
        in_specs=[
            pl.BlockSpec((1, tq, HEAD_DIM), lambda bi, hi, qi: (bi, qi, P_Q * N_HEADS + hi)),
            pl.BlockSpec((1, t_len, HEAD_DIM), lambda bi, hi, qi: (bi, 0, P_K * N_HEADS + hi)),
            pl.BlockSpec((1, t_len, HEAD_DIM), lambda bi, hi, qi: (bi, 0, P_V * N_HEADS + hi)),
            pl.BlockSpec((LANES, HEAD_DIM), lambda bi, hi, qi: (0, P_K * N_HEADS + hi)),
            pl.BlockSpec((LANES, HEAD_DIM), lambda bi, hi, qi: (0, P_V * N_HEADS + hi)),
            pl.BlockSpec((1, 1, LANES), lambda bi, hi, qi: (hi, 0, 0)),
            pl.BlockSpec((1, nq, 1, tq), lambda bi, hi, qi: (bi * N_HEADS + hi, 0, 0, 0)),
        ],
        out_specs=pl.BlockSpec((1, tq, HEAD_DIM), lambda bi, hi, qi: (bi, qi, hi)),
        out_shape=jax.ShapeDtypeStruct((b, t_len, N_HEADS * HEAD_DIM), BF16),
        compiler_params=pltpu.CompilerParams(
            dimension_semantics=("parallel", "parallel", "arbitrary"),
            vmem_limit_bytes=_vmem_limit(blocks, resident)),
        name="attention",
    )(z3, z3, z3, zm_pad, zm_pad, nf_meta, nf_real)


def _merge_kernel(yr_ref, ya_ref, wr_ref, wa_ref, gr_ref, ga_ref, o_ref):
    pr = jnp.dot(yr_ref[...], wr_ref[...], preferred_element_type=F32)
    pa = jnp.dot(ya_ref[...], wa_ref[...], preferred_element_type=F32)
    merged = (jax.nn.sigmoid(gr_ref[...].astype(F32)) * pr
              + jax.nn.sigmoid(ga_ref[...].astype(F32)) * pa)
    o_ref[...] = merged.astype(BF16)


def _merge(y_rnn, y_attn, w_rnn_out, w_attn_out, z, *, tm, tn):
    m = y_rnn.shape[0]
    nn = D_MODEL // tn
    blocks = 2 * tm * D_MODEL * 2 + 2 * D_MODEL * tn * 2 + 3 * tm * tn * 2
    resident = 3 * tm * tn * 4
    return pl.pallas_call(
        _merge_kernel,
        grid=(m // tm, nn),
        in_specs=[
            pl.BlockSpec((tm, D_MODEL), lambda i, j: (i, 0)),
            pl.BlockSpec((tm, D_MODEL), lambda i, j: (i, 0)),
            pl.BlockSpec((D_MODEL, tn), lambda i, j: (0, j)),
            pl.BlockSpec((D_MODEL, tn), lambda i, j: (0, j)),
            pl.BlockSpec((tm, tn), lambda i, j: (i, P_GRNN * nn + j)),
            pl.BlockSpec((tm, tn), lambda i, j: (i, P_GATTN * nn + j)),
        ],
        out_specs=pl.BlockSpec((tm, tn), lambda i, j: (i, j)),
        out_shape=jax.ShapeDtypeStruct((m, D_MODEL), BF16),
        compiler_params=pltpu.CompilerParams(
            dimension_semantics=("parallel", "arbitrary"),
            vmem_limit_bytes=_vmem_limit(blocks, resident)),
        name="merge",
    )(y_rnn, y_attn, w_rnn_out, w_attn_out, z, z)


def _out_proj_kernel(x_ref, w_ref, h_ref, g_ref, o_ref):
    mix = jnp.dot(x_ref[...], w_ref[...], preferred_element_type=F32)
    o_ref[...] = h_ref[...] + _rms(mix, g_ref[...])


def _out_proj(merged, w_o, h, g_post, *, tm):
    m = merged.shape[0]
    blocks = tm * D_MODEL * (2 + 4 + 4) + D_MODEL * D_MODEL * 2
    resident = tm * D_MODEL * 4
    return pl.pallas_call(
        _out_proj_kernel,
        grid=(m // tm,),
        in_specs=[
            pl.BlockSpec((tm, D_MODEL), lambda i: (i, 0)),
            pl.BlockSpec((D_MODEL, D_MODEL), lambda i: (0, 0)),
            pl.BlockSpec((tm, D_MODEL), lambda i: (i, 0)),
            pl.BlockSpec((1, D_MODEL), lambda i: (0, 0)),
        ],
        out_specs=pl.BlockSpec((tm, D_MODEL), lambda i: (i, 0)),
        out_shape=jax.ShapeDtypeStruct((m, D_MODEL), F32),
        compiler_params=pltpu.CompilerParams(
            dimension_semantics=("parallel",),
            vmem_limit_bytes=_vmem_limit(blocks, resident)),
        name="out_proj",
    )(merged, w_o, h, g_post)


def kernel(x, meta_tokens, norm_g, ffn1_w_gu, ffn1_w_down, w_in, conv_w, conv_b, lru_w_a, lru_b_a,
           lru_w_x, lru_b_x, lru_lambda, forget_b, w_rnn_out, w_attn_out, w_o, ffn2_w_gu,
           ffn2_w_down):
    b, t_len, d = x.shape
    m = b * t_len
    g = [norm_g[0, k:k + 1] for k in range(6)]
    row = lambda v: v[0].reshape(1, -1)

    fl_lo = 5 * D_MODEL
    w_main = jnp.concatenate([w_in[0, :, :fl_lo], w_in[0, :, fl_lo + N_HEADS:]], axis=1).astype(BF16)
    w_fl = jnp.pad(w_in[0, :, fl_lo:fl_lo + N_HEADS], ((0, 0), (0, LANES - N_HEADS))).astype(BF16)
    w_ax = jnp.concatenate([lru_w_a[0], lru_w_x[0]], axis=-1).astype(BF16)
    w_gu1, w_d1 = ffn1_w_gu[0].astype(BF16), ffn1_w_down[0].astype(BF16)
    w_gu2, w_d2 = ffn2_w_gu[0].astype(BF16), ffn2_w_down[0].astype(BF16)
    w_r, w_a, w_out = w_rnn_out[0].astype(BF16), w_attn_out[0].astype(BF16), w_o[0].astype(BF16)
    rnn_params = (conv_w[0], row(conv_b), w_ax, row(lru_b_a), row(lru_b_x), row(lru_lambda))

    _, u_m = _ffn(meta_tokens, g[0], w_gu1, w_d1, g[1], g[2], tm=N_META, tf=512)
    z_m, fl_m = _in_proj(u_m, w_main, w_fl, tm=N_META, tn=1024)
    zeros = jnp.zeros((SUBLANES, D_MODEL), F32)
    _, h_meta = _rnn(z_m.reshape(1, N_META, Z_COLS), zeros, zeros[:1], *rnn_params,
                     tt=N_META, cb=512)
    ctx = z_m[N_META - SUBLANES:, :D_MODEL].astype(F32)
    zm_pad = jnp.pad(z_m, ((LANES - N_META, 0), (0, 0)))

    h1, u2 = _ffn(x.reshape(m, d), g[0], w_gu1, w_d1, g[1], g[2], tm=512, tf=512)
    z, fl = _in_proj(u2, w_main, w_fl, tm=1024, tn=1024)
    z3 = z.reshape(b, t_len, Z_COLS)
    y_rnn, _ = _rnn(z3, ctx, h_meta[0, :1], *rnn_params, tt=512, cb=512)

    fl_meta = jnp.pad(fl_m[:, :N_HEADS].T, ((0, 0), (LANES - N_META, 0)))
    fl_real = fl[:, :N_HEADS].reshape(b, t_len, N_HEADS).transpose(0, 2, 1)
    x_all = jnp.concatenate([jnp.broadcast_to(fl_meta[None], (b, N_HEADS, LANES)), fl_real], axis=2)
    neg_f = _forget_prefix(x_all, forget_b[0].reshape(N_HEADS, 1))
    tq = 256
    nf_meta = neg_f[0, :, :LANES].reshape(N_HEADS, 1, LANES)
    nf_real = neg_f[:, :, LANES:].reshape(b * N_HEADS, t_len // tq, 1, tq)
    y_attn = _attention(z3, zm_pad, nf_meta, nf_real, tq=tq)

    merged = _merge(y_rnn.reshape(m, d), y_attn.reshape(m, d), w_r, w_a, z, tm=1024, tn=512)
    h2 = _out_proj(merged, w_out, h1, g[3], tm=512)
    h3, _ = _ffn(h2, g[4], w_gu2, w_d2, g[5], g[5], tm=512, tf=512)
    return h3.reshape(b, t_len, d)
```

```python
import functools

import jax
import jax.numpy as jnp
from jax import lax
from jax.experimental import pallas as pl
from jax.experimental.pallas import tpu as pltpu

D_MODEL = 2048
N_META = 16
D_FF = 5632
N_HEADS = 16
HEAD_DIM = 128
LRU_BS = 128
LRU_BLOCKS = D_MODEL // LRU_BS
CONV_W = 4
LRU_C = 8.0
EPS = 1e-6
N_PIECES = 7
Z_COLS = N_PIECES * D_MODEL
(P_XR, P_GR, P_Q, P_K, P_V, P_GRNN, P_GATTN) = range(N_PIECES)

LANES = 128
SUBLANES = 8
VMEM_BYTES_V7X = 64 * 1024 * 1024
NEG_BIG = -1e30
LOG2E = 1.4426950408889634
QK_SCALE_LOG2 = HEAD_DIM ** -0.5 * LOG2E

F32 = jnp.float32
BF16 = jnp.bfloat16


def _vmem_limit(pipelined_bytes, resident_bytes):
    return min(2 * pipelined_bytes + resident_bytes + (4 << 20), VMEM_BYTES_V7X - (4 << 20))


def _rms(x, g):
    return (x * lax.rsqrt(jnp.mean(x * x, axis=-1, keepdims=True) + EPS)) * g


def _ffn_kernel(h_ref, gpre_ref, wg_ref, wu_ref, wd_ref, gpost_ref, gnext_ref,
                out_ref, unext_ref, u_scr):
    j = pl.program_id(1)

    @pl.when(j == 0)
    def _():
        u_scr[...] = _rms(h_ref[...], gpre_ref[...]).astype(BF16)
        out_ref[...] = jnp.zeros_like(out_ref)

    u = u_scr[...]
    gate = jnp.dot(u, wg_ref[...], preferred_element_type=F32)
    up = jnp.dot(u, wu_ref[...], preferred_element_type=F32)
    act = ((gate * jax.nn.sigmoid(gate)) * up).astype(BF16)
    out_ref[...] += jnp.dot(act, wd_ref[...], preferred_element_type=F32)

    @pl.when(j == pl.num_programs(1) - 1)
    def _():
        h_new = h_ref[...] + 0.5 * _rms(out_ref[...], gpost_ref[...])
        out_ref[...] = h_new
        unext_ref[...] = _rms(h_new, gnext_ref[...]).astype(BF16)


def _ffn(h, g_pre, w_gu, w_down, g_post, g_next, *, tm, tf):
    m = h.shape[0]
    nf = D_FF // tf
    blocks = tm * D_MODEL * (4 + 2) + 3 * D_MODEL * tf * 2
    resident = tm * D_MODEL * (4 + 2) + tm * tf * (4 + 4 + 2)
    return pl.pallas_call(
        _ffn_kernel,
        grid=(m // tm, nf),
        in_specs=[
            pl.BlockSpec((tm, D_MODEL), lambda i, j: (i, 0), pipeline_mode=pl.Buffered(1)),
            pl.BlockSpec((1, D_MODEL), lambda i, j: (0, 0)),
            pl.BlockSpec((D_MODEL, tf), lambda i, j: (0, j)),
            pl.BlockSpec((D_MODEL, tf), lambda i, j: (0, j + nf)),
            pl.BlockSpec((tf, D_MODEL), lambda i, j: (j, 0)),
            pl.BlockSpec((1, D_MODEL), lambda i, j: (0, 0)),
            pl.BlockSpec((1, D_MODEL), lambda i, j: (0, 0)),
        ],
        out_specs=[
            pl.BlockSpec((tm, D_MODEL), lambda i, j: (i, 0)),
            pl.BlockSpec((tm, D_MODEL), lambda i, j: (i, 0)),
        ],
        out_shape=[
            jax.ShapeDtypeStruct((m, D_MODEL), F32),
            jax.ShapeDtypeStruct((m, D_MODEL), BF16),
        ],
        scratch_shapes=[pltpu.VMEM((tm, D_MODEL), BF16)],
        compiler_params=pltpu.CompilerParams(
            dimension_semantics=("parallel", "arbitrary"),
            vmem_limit_bytes=_vmem_limit(blocks, resident)),
        name="ffn",
    )(h, g_pre, w_gu, w_gu, w_down, g_post, g_next)


def _in_proj_kernel(u_ref, w_ref, wfl_ref, z_ref, fl_ref, *, tn):
    j = pl.program_id(1)
    u = u_ref[...]
    is_q = (j >= P_Q * D_MODEL // tn) & (j < (P_Q + 1) * D_MODEL // tn)
    factor = jnp.where(is_q, QK_SCALE_LOG2, 1.0)
    z_ref[...] = (jnp.dot(u, w_ref[...], preferred_element_type=F32) * factor).astype(BF16)

    @pl.when(j == 0)
    def _():
        fl_ref[...] = jnp.dot(u, wfl_ref[...], preferred_element_type=F32)


def _in_proj(u, w_main, w_fl, *, tm, tn):
    m = u.shape[0]
    blocks = tm * D_MODEL * 2 + D_MODEL * tn * 2 + D_MODEL * LANES * 2 + tm * tn * 2 + tm * LANES * 4
    resident = tm * tn * 4
    return pl.pallas_call(
        functools.partial(_in_proj_kernel, tn=tn),
        grid=(m // tm, Z_COLS // tn),
        in_specs=[
            pl.BlockSpec((tm, D_MODEL), lambda i, j: (i, 0)),
            pl.BlockSpec((D_MODEL, tn), lambda i, j: (0, j)),
            pl.BlockSpec((D_MODEL, LANES), lambda i, j: (0, 0)),
        ],
        out_specs=[
            pl.BlockSpec((tm, tn), lambda i, j: (i, j)),
            pl.BlockSpec((tm, LANES), lambda i, j: (i, 0)),
        ],
        out_shape=[
            jax.ShapeDtypeStruct((m, Z_COLS), BF16),
            jax.ShapeDtypeStruct((m, LANES), F32),
        ],
        compiler_params=pltpu.CompilerParams(
            dimension_semantics=("parallel", "arbitrary"),
            vmem_limit_bytes=_vmem_limit(blocks, resident)),
        name="in_proj",
    )(u, w_main, w_fl)


def _rnn_kernel(xr_ref, gr_ref, ctx_ref, h0_ref, cw_ref, cb_ref, wax_ref, ba_ref, bx_ref, lam_ref,
                y_ref, hlast_ref, xbuf, a_scr, b_scr, h_scr, carry_scr, *, tt, cb):
    t = pl.program_id(2)

    @pl.when(t == 0)
    def _():
        xbuf[0:SUBLANES, :] = ctx_ref[...]
        carry_scr[...] = h0_ref[...]

    x = xr_ref[0].astype(F32)
    xbuf[SUBLANES:SUBLANES + tt, :] = x
    xc = xbuf[pl.ds(SUBLANES - 3, tt), :] * cw_ref[0:1, :]
    xc = xc + xbuf[pl.ds(SUBLANES - 2, tt), :] * cw_ref[1:2, :]
    xc = xc + xbuf[pl.ds(SUBLANES - 1, tt), :] * cw_ref[2:3, :]
    xc = xc + x * cw_ref[3:4, :]
    xc = xc + cb_ref[...]
    xbuf[0:SUBLANES, :] = xbuf[tt:tt + SUBLANES, :]

    for n in range(cb // LRU_BS):
        cols = slice(n * LRU_BS, (n + 1) * LRU_BS)
        xn = xc[:, cols]
        g = jnp.dot(xn.astype(BF16), wax_ref[n], preferred_element_type=F32)
        r = jax.nn.sigmoid(g[:, :LRU_BS] + ba_ref[:, cols])
        i = jax.nn.sigmoid(g[:, LRU_BS:] + bx_ref[:, cols])
        neg_lam = -lam_ref[:, cols]
        softplus = jnp.maximum(neg_lam, 0.0) + jnp.log1p(jnp.exp(-jnp.abs(neg_lam)))
        log_a = (-LRU_C * softplus) * r
        a = jnp.exp(log_a)
        a_scr[:, cols] = a
        b_scr[:, cols] = jnp.sqrt(1.0 - a * a) * (i * xn)

    row = lax.broadcasted_iota(jnp.int32, (SUBLANES, cb), 0)

    def group(gi, h_prev):
        rows = pl.ds(pl.multiple_of(gi * SUBLANES, SUBLANES), SUBLANES)
        a8 = a_scr[rows, :]
        b8 = b_scr[rows, :]
        for d in (1, 2, 4):
            live = row >= d
            b8 = jnp.where(live, a8 * pltpu.roll(b8, d, axis=0) + b8, b8)
            a8 = jnp.where(live, a8 * pltpu.roll(a8, d, axis=0), a8)
        h8 = a8 * h_prev + b8
        h_scr[rows, :] = h8
        return h8[SUBLANES - 1:SUBLANES, :]

    h_last = lax.fori_loop(0, tt // SUBLANES, group, carry_scr[...], unroll=4)
    carry_scr[...] = h_last
    hlast_ref[0] = jnp.broadcast_to(h_last, (SUBLANES, cb))
    y_ref[0] = (h_scr[...] * jax.nn.gelu(gr_ref[0].astype(F32), approximate=True)).astype(BF16)


def _rnn(z3, ctx, h0, conv_w, conv_b, w_ax, b_a, b_x, lam, *, tt, cb):
    b, t_len, _ = z3.shape
    ncb = D_MODEL // cb
    nblk = cb // LRU_BS
    vec = lambda rows: pl.BlockSpec((rows, cb), lambda bi, ci, ti: (0, ci))
    blocks = 2 * tt * cb * 2 + tt * cb * 2 + nblk * LRU_BS * 2 * LRU_BS * 2
    resident = tt * cb * 4 * 8
    return pl.pallas_call(
        functools.partial(_rnn_kernel, tt=tt, cb=cb),
        grid=(b, ncb, t_len // tt),
        in_specs=[
            pl.BlockSpec((1, tt, cb), lambda bi, ci, ti: (bi, ti, P_XR * ncb + ci)),
            pl.BlockSpec((1, tt, cb), lambda bi, ci, ti: (bi, ti, P_GR * ncb + ci)),
            vec(SUBLANES), vec(1), vec(CONV_W), vec(1),
            pl.BlockSpec((nblk, LRU_BS, 2 * LRU_BS), lambda bi, ci, ti: (ci, 0, 0)),
            vec(1), vec(1), vec(1),
        ],
        out_specs=[
            pl.BlockSpec((1, tt, cb), lambda bi, ci, ti: (bi, ti, ci)),
            pl.BlockSpec((1, SUBLANES, cb), lambda bi, ci, ti: (bi, 0, ci)),
        ],
        out_shape=[
            jax.ShapeDtypeStruct((b, t_len, D_MODEL), BF16),
            jax.ShapeDtypeStruct((b, SUBLANES, D_MODEL), F32),
        ],
        scratch_shapes=[
            pltpu.VMEM((tt + SUBLANES, cb), F32),
            pltpu.VMEM((tt, cb), F32),
            pltpu.VMEM((tt, cb), F32),
            pltpu.VMEM((tt, cb), F32),
            pltpu.VMEM((1, cb), F32),
        ],
        compiler_params=pltpu.CompilerParams(
            dimension_semantics=("parallel", "parallel", "arbitrary"),
            vmem_limit_bytes=_vmem_limit(blocks, resident)),
        name="rnn",
    )(z3, z3, ctx, h0, conv_w, conv_b, w_ax, b_a, b_x, lam)


def _forget_kernel(x_ref, fb_ref, nf_ref, *, n_blocks, first_valid):
    lane = lax.broadcasted_iota(jnp.int32, (N_HEADS, LANES), 1)
    fb = fb_ref[...]
    run = jnp.zeros((N_HEADS, 1), F32)
    for k in range(n_blocks):
        x = x_ref[0, :, k * LANES:(k + 1) * LANES] + fb
        lf = jnp.minimum(x, 0.0) - jnp.log1p(jnp.exp(-jnp.abs(x)))
        if k == 0:
            lf = jnp.where(lane >= first_valid, lf, 0.0)
        d = 1
        while d < LANES:
            lf = jnp.where(lane >= d, lf + pltpu.roll(lf, d, axis=1), lf)
            d *= 2
        f = lf + run
        run = f[:, LANES - 1:LANES]
        neg = f * -LOG2E
        if k == 0:
            neg = jnp.where(lane >= first_valid, neg, NEG_BIG)
        nf_ref[0, :, k * LANES:(k + 1) * LANES] = neg


def _forget_prefix(x_all, forget_b):
    b, _, n = x_all.shape
    return pl.pallas_call(
        functools.partial(_forget_kernel, n_blocks=n // LANES, first_valid=LANES - N_META),
        grid=(b,),
        in_specs=[
            pl.BlockSpec((1, N_HEADS, n), lambda bi: (bi, 0, 0)),
            pl.BlockSpec((N_HEADS, 1), lambda bi: (0, 0)),
        ],
        out_specs=pl.BlockSpec((1, N_HEADS, n), lambda bi: (bi, 0, 0)),
        out_shape=jax.ShapeDtypeStruct((b, N_HEADS, n), F32),
        compiler_params=pltpu.CompilerParams(dimension_semantics=("parallel",)),
        name="forget_prefix",
    )(x_all, forget_b)


def _attn_kernel(q_ref, k_ref, v_ref, km_ref, vm_ref, nf_ref, o_ref, *, tq, nq, hg):
    qi = pl.program_id(2)
    causal = (lax.broadcasted_iota(jnp.int32, (tq, tq), 0)
              >= lax.broadcasted_iota(jnp.int32, (tq, tq), 1))

    def head(hh, c):
        cols = slice(hh * HEAD_DIM, (hh + 1) * HEAD_DIM)
        q = q_ref[0, :, cols]

        def scores(k_blk, neg_f):
            s = lax.dot_general(q, k_blk, (((1,), (1,)), ((), ())), preferred_element_type=F32)
            return s + neg_f

        lo = c * tq
        ss = [scores(km_ref[:, cols], nf_ref[hh, :, 0:LANES])]
        vs = [vm_ref[:, cols]]
        if c > 0:
            ss.append(scores(k_ref[0, 0:lo, cols], nf_ref[hh, :, LANES:LANES + lo]))
            vs.append(v_ref[0, 0:lo, cols])
        s_diag = scores(k_ref[0, lo:lo + tq, cols], nf_ref[hh, :, LANES + lo:LANES + lo + tq])
        ss.append(jnp.where(causal, s_diag, NEG_BIG))
        vs.append(v_ref[0, lo:lo + tq, cols])
        m = functools.reduce(jnp.maximum, [jnp.max(s, axis=1, keepdims=True) for s in ss])
        l = jnp.zeros((tq, 1), F32)
        o = jnp.zeros((tq, HEAD_DIM), F32)
        for s, v_blk in zip(ss, vs):
            p = jnp.exp2(s - m)
            l = l + jnp.sum(p, axis=1, keepdims=True)
            o = o + jnp.dot(p.astype(BF16), v_blk, preferred_element_type=F32)
        o_ref[0, :, cols] = (o / l).astype(BF16)

    for c in range(nq):
        @pl.when(qi == c)
        def _(c=c):
            for hh in range(hg):
                head(hh, c)


def _attention(z3, zm_pad, neg_f, *, tq, hg):
    b, t_len, _ = z3.shape
    nq = t_len // tq
    ng = N_HEADS // hg
    w = hg * HEAD_DIM
    blocks = (tq * w * 2 * 2 + 2 * t_len * w * 2 + 2 * LANES * w * 2
              + hg * SUBLANES * (LANES + t_len) * 4)
    resident = hg * 4 * tq * (t_len + LANES) * 4
    return pl.pallas_call(
        functools.partial(_attn_kernel, tq=tq, nq=nq, hg=hg),
        grid=(b, ng, nq),
        in_specs=[
            pl.BlockSpec((1, tq, w), lambda bi, gi, qi: (bi, qi, P_Q * ng + gi)),
            pl.BlockSpec((1, t_len, w), lambda bi, gi, qi: (bi, 0, P_K * ng + gi)),
            pl.BlockSpec((1, t_len, w), lambda bi, gi, qi: (bi, 0, P_V * ng + gi)),
            pl.BlockSpec((LANES, w), lambda bi, gi, qi: (0, P_K * ng + gi)),
            pl.BlockSpec((LANES, w), lambda bi, gi, qi: (0, P_V * ng + gi)),
            pl.BlockSpec((hg, 1, LANES + t_len), lambda bi, gi, qi: (bi * ng + gi, 0, 0)),
        ],
        out_specs=pl.BlockSpec((1, tq, w), lambda bi, gi, qi: (bi, qi, gi)),
        out_shape=jax.ShapeDtypeStruct((b, t_len, N_HEADS * HEAD_DIM), BF16),
        compiler_params=pltpu.CompilerParams(
            dimension_semantics=("parallel", "parallel", "arbitrary"),
            vmem_limit_bytes=_vmem_limit(blocks, resident)),
        name="attention",
    )(z3, z3, z3, zm_pad, zm_pad, neg_f)


def _merge_kernel(yr_ref, ya_ref, wr_ref, wa_ref, gr_ref, ga_ref, o_ref):
    pr = jnp.dot(yr_ref[...], wr_ref[...], preferred_element_type=F32)
    pa = jnp.dot(ya_ref[...], wa_ref[...], preferred_element_type=F32)
    merged = (jax.nn.sigmoid(gr_ref[...].astype(F32)) * pr
              + jax.nn.sigmoid(ga_ref[...].astype(F32)) * pa)
    o_ref[...] = merged.astype(BF16)


def _merge(y_rnn, y_attn, w_rnn_out, w_attn_out, z, *, tm, tn):
    m = y_rnn.shape[0]
    nn = D_MODEL // tn
    blocks = 2 * tm * D_MODEL * 2 + 2 * D_MODEL * tn * 2 + 3 * tm * tn * 2
    resident = 3 * tm * tn * 4
    return pl.pallas_call(
        _merge_kernel,
        grid=(m // tm, nn),
        in_specs=[
            pl.BlockSpec((tm, D_MODEL), lambda i, j: (i, 0)),
            pl.BlockSpec((tm, D_MODEL), lambda i, j: (i, 0)),
            pl.BlockSpec((D_MODEL, tn), lambda i, j: (0, j)),
            pl.BlockSpec((D_MODEL, tn), lambda i, j: (0, j)),
            pl.BlockSpec((tm, tn), lambda i, j: (i, P_GRNN * nn + j)),
            pl.BlockSpec((tm, tn), lambda i, j: (i, P_GATTN * nn + j)),
        ],
        out_specs=pl.BlockSpec((tm, tn), lambda i, j: (i, j)),
        out_shape=jax.ShapeDtypeStruct((m, D_MODEL), BF16),
        compiler_params=pltpu.CompilerParams(
            dimension_semantics=("parallel", "arbitrary"),
            vmem_limit_bytes=_vmem_limit(blocks, resident)),
        name="merge",
    )(y_rnn, y_attn, w_rnn_out, w_attn_out, z, z)


def _out_proj_kernel(x_ref, w_ref, h_ref, g_ref, o_ref):
    mix = jnp.dot(x_ref[...], w_ref[...], preferred_element_type=F32)
    o_ref[...] = h_ref[...] + _rms(mix, g_ref[...])


def _out_proj(merged, w_o, h, g_post, *, tm):
    m = merged.shape[0]
    blocks = tm * D_MODEL * (2 + 4 + 4) + D_MODEL * D_MODEL * 2
    resident = tm * D_MODEL * 4
    return pl.pallas_call(
        _out_proj_kernel,
        grid=(m // tm,),
        in_specs=[
            pl.BlockSpec((tm, D_MODEL), lambda i: (i, 0)),
            pl.BlockSpec((D_MODEL, D_MODEL), lambda i: (0, 0)),
            pl.BlockSpec((tm, D_MODEL), lambda i: (i, 0)),
            pl.BlockSpec((1, D_MODEL), lambda i: (0, 0)),
        ],
        out_specs=pl.BlockSpec((tm, D_MODEL), lambda i: (i, 0)),
        out_shape=jax.ShapeDtypeStruct((m, D_MODEL), F32),
        compiler_params=pltpu.CompilerParams(
            dimension_semantics=("parallel",),
            vmem_limit_bytes=_vmem_limit(blocks, resident)),
        name="out_proj",
    )(merged, w_o, h, g_post)


def kernel(x, meta_tokens, norm_g, ffn1_w_gu, ffn1_w_down, w_in, conv_w, conv_b, lru_w_a, lru_b_a,
           lru_w_x, lru_b_x, lru_lambda, forget_b, w_rnn_out, w_attn_out, w_o, ffn2_w_gu,
           ffn2_w_down):
    b, t_len, d = x.shape
    m = b * t_len
    g = [norm_g[0, k:k + 1] for k in range(6)]
    row = lambda v: v[0].reshape(1, -1)

    fl_lo = 5 * D_MODEL
    w_main = jnp.concatenate([w_in[0, :, :fl_lo], w_in[0, :, fl_lo + N_HEADS:]], axis=1).astype(BF16)
    w_fl = jnp.pad(w_in[0, :, fl_lo:fl_lo + N_HEADS], ((0, 0), (0, LANES - N_HEADS))).astype(BF16)
    w_ax = jnp.concatenate([lru_w_a[0], lru_w_x[0]], axis=-1).astype(BF16)
    w_gu1, w_d1 = ffn1_w_gu[0].astype(BF16), ffn1_w_down[0].astype(BF16)
    w_gu2, w_d2 = ffn2_w_gu[0].astype(BF16), ffn2_w_down[0].astype(BF16)
    w_r, w_a, w_out = w_rnn_out[0].astype(BF16), w_attn_out[0].astype(BF16), w_o[0].astype(BF16)
    rnn_params = (conv_w[0], row(conv_b), w_ax, row(lru_b_a), row(lru_b_x), row(lru_lambda))

    _, u_m = _ffn(meta_tokens, g[0], w_gu1, w_d1, g[1], g[2], tm=N_META, tf=512)
    z_m, fl_m = _in_proj(u_m, w_main, w_fl, tm=N_META, tn=1024)
    zeros = jnp.zeros((SUBLANES, D_MODEL), F32)
    _, h_meta = _rnn(z_m.reshape(1, N_META, Z_COLS), zeros, zeros[:1], *rnn_params,
                     tt=N_META, cb=512)
    ctx = z_m[N_META - SUBLANES:, :D_MODEL].astype(F32)
    zm_pad = jnp.pad(z_m, ((LANES - N_META, 0), (0, 0)))

    h1, u2 = _ffn(x.reshape(m, d), g[0], w_gu1, w_d1, g[1], g[2], tm=1024, tf=512)
    z, fl = _in_proj(u2, w_main, w_fl, tm=1024, tn=1024)
    z3 = z.reshape(b, t_len, Z_COLS)
    y_rnn, _ = _rnn(z3, ctx, h_meta[0, :1], *rnn_params, tt=512, cb=512)

    fl_meta = jnp.pad(fl_m[:, :N_HEADS].T, ((0, 0), (LANES - N_META, 0)))
    fl_real = fl[:, :N_HEADS].reshape(b, t_len, N_HEADS).transpose(0, 2, 1)
    x_all = jnp.concatenate([jnp.broadcast_to(fl_meta[None], (b, N_HEADS, LANES)), fl_real], axis=2)
    neg_f = _forget_prefix(x_all, forget_b[0].reshape(N_HEADS, 1))
    y_attn = _attention(z3, zm_pad, neg_f.reshape(b * N_HEADS, 1, LANES + t_len), tq=256, hg=4)

    merged = _merge(y_rnn.reshape(m, d), y_attn.reshape(m, d), w_r, w_a, z, tm=1024, tn=512)
    h2 = _out_proj(merged, w_out, h1, g[3], tm=512)
    h3, _ = _ffn(h2, g[4], w_gu2, w_d2, g[5], g[5], tm=1024, tf=512)
    return h3.reshape(b, t_len, d)
```

```python
import functools

import jax
import jax.numpy as jnp
from jax import lax
from jax.experimental import pallas as pl
from jax.experimental.pallas import tpu as pltpu

D_MODEL = 2048
N_META = 16
D_FF = 5632
N_HEADS = 16
HEAD_DIM = 128
LRU_BS = 128
LRU_BLOCKS = D_MODEL // LRU_BS
CONV_W = 4
LRU_C = 8.0
EPS = 1e-6
(P_XR, P_GR, P_Q, P_K, P_V) = range(5)
Z_COLS = 5 * D_MODEL
(P_GRNN, P_GATTN) = range(2)
ZG_COLS = 2 * D_MODEL
FFN_TF = 512

LANES = 128
SUBLANES = 8
VMEM_BYTES_V7X = 64 * 1024 * 1024
NEG_BIG = -1e30
LOG2E = 1.4426950408889634
QK_SCALE_LOG2 = HEAD_DIM ** -0.5 * LOG2E

F32 = jnp.float32
BF16 = jnp.bfloat16


def _vmem_limit(pipelined_bytes, resident_bytes):
    return min(2 * pipelined_bytes + resident_bytes + (4 << 20), VMEM_BYTES_V7X - (4 << 20))


def _rms(x, g):
    return (x * lax.rsqrt(jnp.mean(x * x, axis=-1, keepdims=True) + EPS)) * g


def _ffn_kernel(h_ref, gpre_ref, wg_ref, wu_ref, wd_ref, gpost_ref, *rest, with_next):
    if with_next:
        gnext_ref, out_ref, unext_ref, u_scr = rest
    else:
        out_ref, u_scr = rest
    j = pl.program_id(1)

    @pl.when(j == 0)
    def _():
        u_scr[...] = _rms(h_ref[...], gpre_ref[...]).astype(BF16)
        out_ref[...] = jnp.zeros_like(out_ref)

    u = u_scr[...]
    gate = jnp.dot(u, wg_ref[...], preferred_element_type=F32)
    up = jnp.dot(u, wu_ref[...], preferred_element_type=F32)
    act = ((gate * jax.nn.sigmoid(gate)) * up).astype(BF16)
    out_ref[...] += jnp.dot(act, wd_ref[...], preferred_element_type=F32)

    @pl.when(j == pl.num_programs(1) - 1)
    def _():
        h_new = h_ref[...] + 0.5 * _rms(out_ref[...], gpost_ref[...])
        out_ref[...] = h_new
        if with_next:
            unext_ref[...] = _rms(h_new, gnext_ref[...]).astype(BF16)


def _ffn(h, g_pre, w_gu, w_down, g_post, g_next, *, tm, h_buffers):
    m = h.shape[0]
    tf = FFN_TF
    nf = D_FF // tf
    with_next = g_next is not None
    row_tile = pl.BlockSpec((tm, D_MODEL), lambda i, j: (i, 0))
    gain = pl.BlockSpec((1, D_MODEL), lambda i, j: (0, 0))
    blocks = tm * D_MODEL * (4 + 2 * with_next) + 3 * D_MODEL * tf * 2
    resident = tm * D_MODEL * (4 * h_buffers + 2 + 4) + tm * tf * (4 + 4 + 2)
    return pl.pallas_call(
        functools.partial(_ffn_kernel, with_next=with_next),
        grid=(m // tm, nf),
        in_specs=[
            pl.BlockSpec((tm, D_MODEL), lambda i, j: (i, 0), pipeline_mode=pl.Buffered(h_buffers)),
            gain,
            pl.BlockSpec((D_MODEL, tf), lambda i, j: (0, j)),
            pl.BlockSpec((D_MODEL, tf), lambda i, j: (0, j + nf)),
            pl.BlockSpec((tf, D_MODEL), lambda i, j: (j, 0)),
            gain,
        ] + [gain] * with_next,
        out_specs=[row_tile] * (1 + with_next),
        out_shape=[jax.ShapeDtypeStruct((m, D_MODEL), F32)]
        + [jax.ShapeDtypeStruct((m, D_MODEL), BF16)] * with_next,
        scratch_shapes=[pltpu.VMEM((tm, D_MODEL), BF16)],
        compiler_params=pltpu.CompilerParams(
            dimension_semantics=("parallel", "arbitrary"),
            vmem_limit_bytes=_vmem_limit(blocks, resident)),
        name="ffn",
    )(h, g_pre, w_gu, w_gu, w_down, g_post, *([g_next] * with_next))


def _dot_nt(x, wt):
    return lax.dot_general(x, wt, (((1,), (1,)), ((), ())), preferred_element_type=F32)


def _in_proj_kernel(u_ref, wt_ref, z_ref, *, tn):
    j = pl.program_id(0)
    is_q = (j >= P_Q * D_MODEL // tn) & (j < (P_Q + 1) * D_MODEL // tn)
    factor = jnp.where(is_q, QK_SCALE_LOG2, 1.0)
    z_ref[...] = (_dot_nt(u_ref[...], wt_ref[...]) * factor).astype(BF16)


def _in_proj(u, w_in_t, *, tm, tn):
    m = u.shape[0]
    blocks = tm * D_MODEL * 2 + D_MODEL * tn * 2 + tm * tn * 2
    resident = tm * tn * 4
    return pl.pallas_call(
        functools.partial(_in_proj_kernel, tn=tn),
        grid=(Z_COLS // tn, m // tm),
        in_specs=[
            pl.BlockSpec((tm, D_MODEL), lambda j, i: (i, 0)),
            pl.BlockSpec((tn, D_MODEL), lambda j, i: (j, 0)),
        ],
        out_specs=pl.BlockSpec((tm, tn), lambda j, i: (i, j)),
        out_shape=jax.ShapeDtypeStruct((m, Z_COLS), BF16),
        compiler_params=pltpu.CompilerParams(
            dimension_semantics=("parallel", "parallel"),
            vmem_limit_bytes=_vmem_limit(blocks, resident)),
        name="in_proj",
    )(u, w_in_t)


def _gate_proj_kernel(u_ref, wt_ref, wflt_ref, zg_ref, fl_ref):
    u = u_ref[...]
    zg_ref[...] = _dot_nt(u, wt_ref[...]).astype(BF16)

    @pl.when(pl.program_id(1) == 0)
    def _():
        fl_ref[...] = _dot_nt(u, wflt_ref[...])


def _gate_proj(u, w_hi_t, w_fl_t, *, tm, tn):
    m = u.shape[0]
    blocks = tm * D_MODEL * 2 + D_MODEL * tn * 2 + D_MODEL * LANES * 2 + tm * tn * 2 + tm * LANES * 4
    resident = tm * tn * 4
    return pl.pallas_call(
        _gate_proj_kernel,
        grid=(m // tm, ZG_COLS // tn),
        in_specs=[
            pl.BlockSpec((tm, D_MODEL), lambda i, j: (i, 0)),
            pl.BlockSpec((tn, D_MODEL), lambda i, j: (j, 0)),
            pl.BlockSpec((LANES, D_MODEL), lambda i, j: (0, 0)),
        ],
        out_specs=[
            pl.BlockSpec((tm, tn), lambda i, j: (i, j)),
            pl.BlockSpec((tm, LANES), lambda i, j: (i, 0)),
        ],
        out_shape=[
            jax.ShapeDtypeStruct((m, ZG_COLS), BF16),
            jax.ShapeDtypeStruct((m, LANES), F32),
        ],
        compiler_params=pltpu.CompilerParams(
            dimension_semantics=("parallel", "arbitrary"),
            vmem_limit_bytes=_vmem_limit(blocks, resident)),
        name="gate_proj",
    )(u, w_hi_t, w_fl_t)


def _rnn_kernel(xr_ref, gr_ref, ctx_ref, h0_ref, cw_ref, cb_ref, wax_ref, ba_ref, bx_ref, lam_ref,
                y_ref, hlast_ref, xbuf, a_scr, b_scr, h_scr, carry_scr, *, tt, cb):
    t = pl.program_id(2)

    @pl.when(t == 0)
    def _():
        xbuf[0:SUBLANES, :] = ctx_ref[...]
        carry_scr[...] = h0_ref[...]

    x = xr_ref[0].astype(F32)
    xbuf[SUBLANES:SUBLANES + tt, :] = x
    xc = xbuf[pl.ds(SUBLANES - 3, tt), :] * cw_ref[0:1, :]
    xc = xc + xbuf[pl.ds(SUBLANES - 2, tt), :] * cw_ref[1:2, :]
    xc = xc + xbuf[pl.ds(SUBLANES - 1, tt), :] * cw_ref[2:3, :]
    xc = xc + x * cw_ref[3:4, :]
    xc = xc + cb_ref[...]
    xbuf[0:SUBLANES, :] = xbuf[tt:tt + SUBLANES, :]

    for n in range(cb // LRU_BS):
        cols = slice(n * LRU_BS, (n + 1) * LRU_BS)
        xn = xc[:, cols]
        g = jnp.dot(xn.astype(BF16), wax_ref[n], preferred_element_type=F32)
        r = jax.nn.sigmoid(g[:, :LRU_BS] + ba_ref[:, cols])
        i = jax.nn.sigmoid(g[:, LRU_BS:] + bx_ref[:, cols])
        neg_lam = -lam_ref[:, cols]
        softplus = jnp.maximum(neg_lam, 0.0) + jnp.log1p(jnp.exp(-jnp.abs(neg_lam)))
        a = jnp.exp2((-LRU_C * LOG2E * softplus) * r)
        a_scr[:, cols] = a
        b_scr[:, cols] = jnp.sqrt(1.0 - a * a) * (i * xn)

    row = lax.broadcasted_iota(jnp.int32, (SUBLANES, cb), 0)

    def group(gi, h_prev):
        rows = pl.ds(pl.multiple_of(gi * SUBLANES, SUBLANES), SUBLANES)
        a8 = a_scr[rows, :]
        b8 = b_scr[rows, :]
        for d in (1, 2, 4):
            live = row >= d
            b8 = jnp.where(live, a8 * pltpu.roll(b8, d, axis=0) + b8, b8)
            a8 = jnp.where(live, a8 * pltpu.roll(a8, d, axis=0), a8)
        h8 = a8 * h_prev + b8
        h_scr[rows, :] = h8
        return h8[SUBLANES - 1:SUBLANES, :]

    h_last = lax.fori_loop(0, tt // SUBLANES, group, carry_scr[...], unroll=4)
    carry_scr[...] = h_last
    hlast_ref[0] = jnp.broadcast_to(h_last, (SUBLANES, cb))
    y_ref[0] = (h_scr[...] * jax.nn.gelu(gr_ref[0].astype(F32), approximate=True)).astype(BF16)


def _rnn(z3, ctx, h0, conv_w, conv_b, w_ax, b_a, b_x, lam, *, tt, cb):
    b, t_len, _ = z3.shape
    ncb = D_MODEL // cb
    nblk = cb // LRU_BS
    vec = lambda rows: pl.BlockSpec((rows, cb), lambda bi, ci, ti: (0, ci))
    blocks = 2 * tt * cb * 2 + tt * cb * 2 + nblk * LRU_BS * 2 * LRU_BS * 2
    resident = tt * cb * 4 * 8
    return pl.pallas_call(
        functools.partial(_rnn_kernel, tt=tt, cb=cb),
        grid=(b, ncb, t_len // tt),
        in_specs=[
            pl.BlockSpec((1, tt, cb), lambda bi, ci, ti: (bi, ti, P_XR * ncb + ci)),
            pl.BlockSpec((1, tt, cb), lambda bi, ci, ti: (bi, ti, P_GR * ncb + ci)),
            vec(SUBLANES), vec(1), vec(CONV_W), vec(1),
            pl.BlockSpec((nblk, LRU_BS, 2 * LRU_BS), lambda bi, ci, ti: (ci, 0, 0)),
            vec(1), vec(1), vec(1),
        ],
        out_specs=[
            pl.BlockSpec((1, tt, cb), lambda bi, ci, ti: (bi, ti, ci)),
            pl.BlockSpec((1, SUBLANES, cb), lambda bi, ci, ti: (bi, 0, ci)),
        ],
        out_shape=[
            jax.ShapeDtypeStruct((b, t_len, D_MODEL), BF16),
            jax.ShapeDtypeStruct((b, SUBLANES, D_MODEL), F32),
        ],
        scratch_shapes=[
            pltpu.VMEM((tt + SUBLANES, cb), F32),
            pltpu.VMEM((tt, cb), F32),
            pltpu.VMEM((tt, cb), F32),
            pltpu.VMEM((tt, cb), F32),
            pltpu.VMEM((1, cb), F32),
        ],
        compiler_params=pltpu.CompilerParams(
            dimension_semantics=("parallel", "parallel", "arbitrary"),
            vmem_limit_bytes=_vmem_limit(blocks, resident)),
        name="rnn",
    )(z3, z3, ctx, h0, conv_w, conv_b, w_ax, b_a, b_x, lam)


def _forget_kernel(x_ref, fb_ref, nf_ref, *, n_blocks, first_valid):
    lane = lax.broadcasted_iota(jnp.int32, (N_HEADS, LANES), 1)
    fb = fb_ref[...]
    run = jnp.zeros((N_HEADS, 1), F32)
    for k in range(n_blocks):
        x = x_ref[0, :, k * LANES:(k + 1) * LANES] + fb
        lf = jnp.minimum(x, 0.0) - jnp.log1p(jnp.exp(-jnp.abs(x)))
        if k == 0:
            lf = jnp.where(lane >= first_valid, lf, 0.0)
        d = 1
        while d < LANES:
            lf = jnp.where(lane >= d, lf + pltpu.roll(lf, d, axis=1), lf)
            d *= 2
        f = lf + run
        run = f[:, LANES - 1:LANES]
        neg = f * -LOG2E
        if k == 0:
            neg = jnp.where(lane >= first_valid, neg, NEG_BIG)
        nf_ref[0, :, k * LANES:(k + 1) * LANES] = neg


def _forget_prefix(x_all, forget_b):
    b, _, n = x_all.shape
    return pl.pallas_call(
        functools.partial(_forget_kernel, n_blocks=n // LANES, first_valid=LANES - N_META),
        grid=(b,),
        in_specs=[
            pl.BlockSpec((1, N_HEADS, n), lambda bi: (bi, 0, 0)),
            pl.BlockSpec((N_HEADS, 1), lambda bi: (0, 0)),
        ],
        out_specs=pl.BlockSpec((1, N_HEADS, n), lambda bi: (bi, 0, 0)),
        out_shape=jax.ShapeDtypeStruct((b, N_HEADS, n), F32),
        compiler_params=pltpu.CompilerParams(dimension_semantics=("parallel",)),
        name="forget_prefix",
    )(x_all, forget_b)


def _attn_kernel(q_ref, k_ref, v_ref, km_ref, vm_ref, nf_ref, o_ref, *, tq, nq, hg):
    qi = pl.program_id(2)
    causal = (lax.broadcasted_iota(jnp.int32, (tq, tq), 0)
              >= lax.broadcasted_iota(jnp.int32, (tq, tq), 1))

    def head(hh, c):
        cols = slice(hh * HEAD_DIM, (hh + 1) * HEAD_DIM)
        q = q_ref[0, :, cols]

        def scores(k_blk, neg_f):
            s = lax.dot_general(q, k_blk, (((1,), (1,)), ((), ())), preferred_element_type=F32)
            return s + neg_f

        lo = c * tq
        ss = [scores(km_ref[:, cols], nf_ref[hh, :, 0:LANES])]
        vs = [vm_ref[:, cols]]
        if c > 0:
            ss.append(scores(k_ref[0, 0:lo, cols], nf_ref[hh, :, LANES:LANES + lo]))
            vs.append(v_ref[0, 0:lo, cols])
        s_diag = scores(k_ref[0, lo:lo + tq, cols], nf_ref[hh, :, LANES + lo:LANES + lo + tq])
        ss.append(jnp.where(causal, s_diag, NEG_BIG))
        vs.append(v_ref[0, lo:lo + tq, cols])
        m = functools.reduce(jnp.maximum, [jnp.max(s, axis=1, keepdims=True) for s in ss])
        l = jnp.zeros((tq, 1), F32)
        o = jnp.zeros((tq, HEAD_DIM), F32)
        for s, v_blk in zip(ss, vs):
            p = jnp.exp2(s - m)
            l = l + jnp.sum(p, axis=1, keepdims=True)
            o = o + jnp.dot(p.astype(BF16), v_blk, preferred_element_type=F32)
        o_ref[0, :, cols] = (o / l).astype(BF16)

    for c in range(nq):
        @pl.when(qi == c)
        def _(c=c):
            for hh in range(hg):
                head(hh, c)


def _attention(z3, zm_pad, neg_f, *, tq, hg):
    b, t_len, _ = z3.shape
    nq = t_len // tq
    ng = N_HEADS // hg
    w = hg * HEAD_DIM
    blocks = (tq * w * 2 * 2 + 2 * t_len * w * 2 + 2 * LANES * w * 2
              + hg * SUBLANES * (LANES + t_len) * 4)
    resident = hg * 4 * tq * (t_len + LANES) * 4
    return pl.pallas_call(
        functools.partial(_attn_kernel, tq=tq, nq=nq, hg=hg),
        grid=(b, ng, nq),
        in_specs=[
            pl.BlockSpec((1, tq, w), lambda bi, gi, qi: (bi, qi, P_Q * ng + gi)),
            pl.BlockSpec((1, t_len, w), lambda bi, gi, qi: (bi, 0, P_K * ng + gi)),
            pl.BlockSpec((1, t_len, w), lambda bi, gi, qi: (bi, 0, P_V * ng + gi)),
            pl.BlockSpec((LANES, w), lambda bi, gi, qi: (0, P_K * ng + gi)),
            pl.BlockSpec((LANES, w), lambda bi, gi, qi: (0, P_V * ng + gi)),
            pl.BlockSpec((hg, 1, LANES + t_len), lambda bi, gi, qi: (bi * ng + gi, 0, 0)),
        ],
        out_specs=pl.BlockSpec((1, tq, w), lambda bi, gi, qi: (bi, qi, gi)),
        out_shape=jax.ShapeDtypeStruct((b, t_len, N_HEADS * HEAD_DIM), BF16),
        compiler_params=pltpu.CompilerParams(
            dimension_semantics=("parallel", "parallel", "arbitrary"),
            vmem_limit_bytes=_vmem_limit(blocks, resident)),
        name="attention",
    )(z3, z3, z3, zm_pad, zm_pad, neg_f)


def _merge_kernel(yr_ref, ya_ref, wr_ref, wa_ref, gr_ref, ga_ref, o_ref):
    pr = jnp.dot(yr_ref[...], wr_ref[...], preferred_element_type=F32)
    pa = jnp.dot(ya_ref[...], wa_ref[...], preferred_element_type=F32)
    merged = (jax.nn.sigmoid(gr_ref[...].astype(F32)) * pr
              + jax.nn.sigmoid(ga_ref[...].astype(F32)) * pa)
    o_ref[...] = merged.astype(BF16)


def _merge(y_rnn, y_attn, w_rnn_out, w_attn_out, zg, *, tm, tn):
    m = y_rnn.shape[0]
    nn = D_MODEL // tn
    blocks = 2 * tm * D_MODEL * 2 + 2 * D_MODEL * tn * 2 + 3 * tm * tn * 2
    resident = 3 * tm * tn * 4
    return pl.pallas_call(
        _merge_kernel,
        grid=(m // tm, nn),
        in_specs=[
            pl.BlockSpec((tm, D_MODEL), lambda i, j: (i, 0)),
            pl.BlockSpec((tm, D_MODEL), lambda i, j: (i, 0)),
            pl.BlockSpec((D_MODEL, tn), lambda i, j: (0, j)),
            pl.BlockSpec((D_MODEL, tn), lambda i, j: (0, j)),
            pl.BlockSpec((tm, tn), lambda i, j: (i, P_GRNN * nn + j)),
            pl.BlockSpec((tm, tn), lambda i, j: (i, P_GATTN * nn + j)),
        ],
        out_specs=pl.BlockSpec((tm, tn), lambda i, j: (i, j)),
        out_shape=jax.ShapeDtypeStruct((m, D_MODEL), BF16),
        compiler_params=pltpu.CompilerParams(
            dimension_semantics=("parallel", "arbitrary"),
            vmem_limit_bytes=_vmem_limit(blocks, resident)),
        name="merge",
    )(y_rnn, y_attn, w_rnn_out, w_attn_out, zg, zg)


def _out_proj_kernel(x_ref, w_ref, h_ref, g_ref, o_ref):
    mix = jnp.dot(x_ref[...], w_ref[...], preferred_element_type=F32)
    o_ref[...] = h_ref[...] + _rms(mix, g_ref[...])


def _out_proj(merged, w_o, h, g_post, *, tm):
    m = merged.shape[0]
    blocks = tm * D_MODEL * (2 + 4 + 4) + D_MODEL * D_MODEL * 2
    resident = tm * D_MODEL * 4
    return pl.pallas_call(
        _out_proj_kernel,
        grid=(m // tm,),
        in_specs=[
            pl.BlockSpec((tm, D_MODEL), lambda i: (i, 0)),
            pl.BlockSpec((D_MODEL, D_MODEL), lambda i: (0, 0)),
            pl.BlockSpec((tm, D_MODEL), lambda i: (i, 0)),
            pl.BlockSpec((1, D_MODEL), lambda i: (0, 0)),
        ],
        out_specs=pl.BlockSpec((tm, D_MODEL), lambda i: (i, 0)),
        out_shape=jax.ShapeDtypeStruct((m, D_MODEL), F32),
        compiler_params=pltpu.CompilerParams(
            dimension_semantics=("parallel",),
            vmem_limit_bytes=_vmem_limit(blocks, resident)),
        name="out_proj",
    )(merged, w_o, h, g_post)


def kernel(x, meta_tokens, norm_g, ffn1_w_gu, ffn1_w_down, w_in, conv_w, conv_b, lru_w_a, lru_b_a,
           lru_w_x, lru_b_x, lru_lambda, forget_b, w_rnn_out, w_attn_out, w_o, ffn2_w_gu,
           ffn2_w_down):
    b, t_len, d = x.shape
    m = b * t_len
    g = [norm_g[0, k:k + 1] for k in range(6)]
    row = lambda v: v[0].reshape(1, -1)

    w_in_t = w_in[0].T.astype(BF16)
    w_fl_t = w_in_t[Z_COLS:Z_COLS + LANES]
    w_hi_t = w_in_t[Z_COLS + N_HEADS:]
    w_ax = jnp.concatenate([lru_w_a[0], lru_w_x[0]], axis=-1).astype(BF16)
    w_gu1, w_d1 = ffn1_w_gu[0].astype(BF16), ffn1_w_down[0].astype(BF16)
    w_gu2, w_d2 = ffn2_w_gu[0].astype(BF16), ffn2_w_down[0].astype(BF16)
    w_r, w_a, w_out = w_rnn_out[0].astype(BF16), w_attn_out[0].astype(BF16), w_o[0].astype(BF16)
    rnn_params = (conv_w[0], row(conv_b), w_ax, row(lru_b_a), row(lru_b_x), row(lru_lambda))

    _, u_m = _ffn(meta_tokens, g[0], w_gu1, w_d1, g[1], g[2], tm=N_META, h_buffers=2)
    z_m = _in_proj(u_m, w_in_t, tm=N_META, tn=1024)
    _, fl_m = _gate_proj(u_m, w_hi_t, w_fl_t, tm=N_META, tn=1024)
    zeros = jnp.zeros((SUBLANES, D_MODEL), F32)
    _, h_meta = _rnn(z_m.reshape(1, N_META, Z_COLS), zeros, zeros[:1], *rnn_params,
                     tt=N_META, cb=512)
    ctx = z_m[N_META - SUBLANES:, :D_MODEL].astype(F32)
    zm_pad = jnp.pad(z_m, ((LANES - N_META, 0), (0, 0)))

    h1, u2 = _ffn(x.reshape(m, d), g[0], w_gu1, w_d1, g[1], g[2], tm=1024, h_buffers=1)
    z = _in_proj(u2, w_in_t, tm=1024, tn=1024)
    zg, fl = _gate_proj(u2, w_hi_t, w_fl_t, tm=1024, tn=1024)
    z3 = z.reshape(b, t_len, Z_COLS)
    y_rnn, _ = _rnn(z3, ctx, h_meta[0, :1], *rnn_params, tt=512, cb=512)

    fl_meta = jnp.pad(fl_m[:, :N_HEADS].T, ((0, 0), (LANES - N_META, 0)))
    fl_real = fl[:, :N_HEADS].reshape(b, t_len, N_HEADS).transpose(0, 2, 1)
    x_all = jnp.concatenate([jnp.broadcast_to(fl_meta[None], (b, N_HEADS, LANES)), fl_real], axis=2)
    neg_f = _forget_prefix(x_all, forget_b[0].reshape(N_HEADS, 1))
    y_attn = _attention(z3, zm_pad, neg_f.reshape(b * N_HEADS, 1, LANES + t_len), tq=512, hg=4)

    merged = _merge(y_rnn.reshape(m, d), y_attn.reshape(m, d), w_r, w_a, zg, tm=1024, tn=512)
    h2 = _out_proj(merged, w_out, h1, g[3], tm=512)
    (h3,) = _ffn(h2, g[4], w_gu2, w_d2, g[5], None, tm=512, h_buffers=2)
    return h3.reshape(b, t_len, d)
```

```python
import functools

import jax
import jax.numpy as jnp
from jax import lax
from jax.experimental import pallas as pl
from jax.experimental.pallas import tpu as pltpu

D_MODEL = 2048
N_META = 16
D_FF = 5632
N_HEADS = 16
HEAD_DIM = 128
LRU_BS = 128
LRU_BLOCKS = D_MODEL // LRU_BS
CONV_W = 4
LRU_C = 8.0
EPS = 1e-6
(P_XR, P_GR, P_Q, P_K, P_V) = range(5)
Z_COLS = 5 * D_MODEL
(P_GRNN, P_GATTN) = range(2)
ZG_COLS = 2 * D_MODEL
FFN_TF = 512

LANES = 128
SUBLANES = 8
VMEM_BYTES_V7X = 64 * 1024 * 1024
NEG_BIG = -1e30
LOG2E = 1.4426950408889634
QK_SCALE_LOG2 = HEAD_DIM ** -0.5 * LOG2E

F32 = jnp.float32
BF16 = jnp.bfloat16


def _vmem_limit(pipelined_bytes, resident_bytes):
    return min(2 * pipelined_bytes + resident_bytes + (4 << 20), VMEM_BYTES_V7X - (4 << 20))


def _rms(x, g):
    return (x * lax.rsqrt(jnp.mean(x * x, axis=-1, keepdims=True) + EPS)) * g


def _ffn_kernel(h_ref, gpre_ref, wg_ref, wu_ref, wd_ref, gpost_ref, *rest, with_next):
    if with_next:
        gnext_ref, out_ref, unext_ref, u_scr = rest
    else:
        out_ref, u_scr = rest
    j = pl.program_id(1)

    @pl.when(j == 0)
    def _():
        u_scr[...] = _rms(h_ref[...], gpre_ref[...]).astype(BF16)
        out_ref[...] = jnp.zeros_like(out_ref)

    u = u_scr[...]
    gate = jnp.dot(u, wg_ref[...], preferred_element_type=F32)
    up = jnp.dot(u, wu_ref[...], preferred_element_type=F32)
    act = ((gate * jax.nn.sigmoid(gate)) * up).astype(BF16)
    out_ref[...] += jnp.dot(act, wd_ref[...], preferred_element_type=F32)

    @pl.when(j == pl.num_programs(1) - 1)
    def _():
        h_new = h_ref[...] + 0.5 * _rms(out_ref[...], gpost_ref[...])
        out_ref[...] = h_new
        if with_next:
            unext_ref[...] = _rms(h_new, gnext_ref[...]).astype(BF16)


def _ffn(h, g_pre, w_gu, w_down, g_post, g_next, *, tm, h_buffers):
    m = h.shape[0]
    tf = FFN_TF
    nf = D_FF // tf
    with_next = g_next is not None
    row_tile = pl.BlockSpec((tm, D_MODEL), lambda i, j: (i, 0))
    gain = pl.BlockSpec((1, D_MODEL), lambda i, j: (0, 0))
    blocks = tm * D_MODEL * (4 + 2 * with_next) + 3 * D_MODEL * tf * 2
    resident = tm * D_MODEL * (4 * h_buffers + 2 + 4) + tm * tf * (4 + 4 + 2)
    return pl.pallas_call(
        functools.partial(_ffn_kernel, with_next=with_next),
        grid=(m // tm, nf),
        in_specs=[
            pl.BlockSpec((tm, D_MODEL), lambda i, j: (i, 0), pipeline_mode=pl.Buffered(h_buffers)),
            gain,
            pl.BlockSpec((D_MODEL, tf), lambda i, j: (0, j)),
            pl.BlockSpec((D_MODEL, tf), lambda i, j: (0, j + nf)),
            pl.BlockSpec((tf, D_MODEL), lambda i, j: (j, 0)),
            gain,
        ] + [gain] * with_next,
        out_specs=[row_tile] * (1 + with_next),
        out_shape=[jax.ShapeDtypeStruct((m, D_MODEL), F32)]
        + [jax.ShapeDtypeStruct((m, D_MODEL), BF16)] * with_next,
        scratch_shapes=[pltpu.VMEM((tm, D_MODEL), BF16)],
        compiler_params=pltpu.CompilerParams(
            dimension_semantics=("parallel", "arbitrary"),
            vmem_limit_bytes=_vmem_limit(blocks, resident)),
        name="ffn",
    )(h, g_pre, w_gu, w_gu, w_down, g_post, *([g_next] * with_next))


def _dot_nt(x, wt):
    return lax.dot_general(x, wt, (((1,), (1,)), ((), ())), preferred_element_type=F32)


def _in_proj_kernel(u_ref, wt_ref, z_ref, *, tn):
    j = pl.program_id(0)
    is_q = (j >= P_Q * D_MODEL // tn) & (j < (P_Q + 1) * D_MODEL // tn)
    factor = jnp.where(is_q, QK_SCALE_LOG2, 1.0)
    z_ref[...] = (_dot_nt(u_ref[...], wt_ref[...]) * factor).astype(BF16)


def _in_proj(u, w_in_t, *, tm, tn):
    m = u.shape[0]
    blocks = tm * D_MODEL * 2 + D_MODEL * tn * 2 + tm * tn * 2
    resident = tm * tn * 4
    return pl.pallas_call(
        functools.partial(_in_proj_kernel, tn=tn),
        grid=(Z_COLS // tn, m // tm),
        in_specs=[
            pl.BlockSpec((tm, D_MODEL), lambda j, i: (i, 0)),
            pl.BlockSpec((tn, D_MODEL), lambda j, i: (j, 0)),
        ],
        out_specs=pl.BlockSpec((tm, tn), lambda j, i: (i, j)),
        out_shape=jax.ShapeDtypeStruct((m, Z_COLS), BF16),
        compiler_params=pltpu.CompilerParams(
            dimension_semantics=("parallel", "parallel"),
            vmem_limit_bytes=_vmem_limit(blocks, resident)),
        name="in_proj",
    )(u, w_in_t)


def _gate_proj_kernel(u_ref, wt_ref, wflt_ref, zg_ref, fl_ref):
    u = u_ref[...]
    zg_ref[...] = _dot_nt(u, wt_ref[...]).astype(BF16)

    @pl.when(pl.program_id(1) == 0)
    def _():
        fl_ref[...] = _dot_nt(u, wflt_ref[...])


def _gate_proj(u, w_hi_t, w_fl_t, *, tm, tn):
    m = u.shape[0]
    blocks = tm * D_MODEL * 2 + D_MODEL * tn * 2 + D_MODEL * LANES * 2 + tm * tn * 2 + tm * LANES * 4
    resident = tm * tn * 4
    return pl.pallas_call(
        _gate_proj_kernel,
        grid=(m // tm, ZG_COLS // tn),
        in_specs=[
            pl.BlockSpec((tm, D_MODEL), lambda i, j: (i, 0)),
            pl.BlockSpec((tn, D_MODEL), lambda i, j: (j, 0)),
            pl.BlockSpec((LANES, D_MODEL), lambda i, j: (0, 0)),
        ],
        out_specs=[
            pl.BlockSpec((tm, tn), lambda i, j: (i, j)),
            pl.BlockSpec((tm, LANES), lambda i, j: (i, 0)),
        ],
        out_shape=[
            jax.ShapeDtypeStruct((m, ZG_COLS), BF16),
            jax.ShapeDtypeStruct((m, LANES), F32),
        ],
        compiler_params=pltpu.CompilerParams(
            dimension_semantics=("parallel", "arbitrary"),
            vmem_limit_bytes=_vmem_limit(blocks, resident)),
        name="gate_proj",
    )(u, w_hi_t, w_fl_t)


def _rnn_kernel(xr_ref, gr_ref, ctx_ref, h0_ref, cw_ref, cb_ref, wax_ref, ba_ref, bx_ref, lam_ref,
                y_ref, hlast_ref, xbuf, a_scr, b_scr, h_scr, carry_scr, *, tt, cb, pitch):
    t = pl.program_id(1)
    nslab = cb // LRU_BS

    @pl.when(t == 0)
    def _():
        xbuf[:, 0:SUBLANES, :] = jnp.broadcast_to(ctx_ref[...][None], (SUBLANES, SUBLANES, cb))
        carry_scr[...] = jnp.broadcast_to(h0_ref[...], (SUBLANES, cb))

    x = xr_ref[...].astype(F32)
    xbuf[:, SUBLANES:SUBLANES + tt, :] = x
    xc = xbuf[:, pl.ds(SUBLANES - 3, tt), :] * cw_ref[0:1, :]
    xc = xc + xbuf[:, pl.ds(SUBLANES - 2, tt), :] * cw_ref[1:2, :]
    xc = xc + xbuf[:, pl.ds(SUBLANES - 1, tt), :] * cw_ref[2:3, :]
    xc = xc + x * cw_ref[3:4, :]
    xc = xc + cb_ref[...]
    xbuf[:, 0:SUBLANES, :] = xbuf[:, tt:tt + SUBLANES, :]

    for n in range(nslab):
        cols = slice(n * LRU_BS, (n + 1) * LRU_BS)
        xn = xc[:, :, cols].reshape(SUBLANES * tt, LRU_BS)
        g = jnp.dot(xn.astype(BF16), wax_ref[n], preferred_element_type=F32)
        r = jax.nn.sigmoid(g[:, :LRU_BS] + ba_ref[:, cols])
        i = jax.nn.sigmoid(g[:, LRU_BS:] + bx_ref[:, cols])
        neg_lam = -lam_ref[:, cols]
        softplus = jnp.maximum(neg_lam, 0.0) + jnp.log1p(jnp.exp(-jnp.abs(neg_lam)))
        a = jnp.exp2((-LRU_C * LOG2E * softplus) * r)
        bv = jnp.sqrt(1.0 - a * a) * (i * xn)
        for b in range(SUBLANES):
            a_scr[n, b * pitch:b * pitch + tt, :] = a[b * tt:(b + 1) * tt]
            b_scr[n, b * pitch:b * pitch + tt, :] = bv[b * tt:(b + 1) * tt]

    def step(tl, hs):
        rows = pl.ds(tl, SUBLANES, stride=pitch)
        hs = tuple(a_scr[n, rows, :] * hs[n] + b_scr[n, rows, :] for n in range(nslab))
        for n in range(nslab):
            h_scr[n, rows, :] = hs[n]
        return hs

    hs = tuple(carry_scr[:, n * LRU_BS:(n + 1) * LRU_BS] for n in range(nslab))
    hs = lax.fori_loop(0, tt, step, hs, unroll=8)
    for n in range(nslab):
        cols = slice(n * LRU_BS, (n + 1) * LRU_BS)
        carry_scr[:, cols] = hs[n]
        hlast_ref[:, cols] = hs[n]
        for b in range(SUBLANES):
            gate = jax.nn.gelu(gr_ref[b, :, cols].astype(F32), approximate=True)
            y_ref[b, :, cols] = (h_scr[n, b * pitch:b * pitch + tt, :] * gate).astype(BF16)


def _rnn(z3, ctx, h0, conv_w, conv_b, w_ax, b_a, b_x, lam, *, tt, cb):
    b, t_len, _ = z3.shape
    assert b == SUBLANES
    ncb = D_MODEL // cb
    nblk = cb // LRU_BS
    pitch = tt + SUBLANES if (tt // SUBLANES) % 2 == 0 else tt + 2 * SUBLANES
    vec = lambda rows: pl.BlockSpec((rows, cb), lambda ci, ti: (0, ci))
    blocks = 3 * b * tt * cb * 2 + nblk * LRU_BS * 2 * LRU_BS * 2 + b * cb * 4
    resident = b * (tt + SUBLANES) * cb * 4 + 3 * b * pitch * cb * 4 + 6 * b * tt * cb * 4
    return pl.pallas_call(
        functools.partial(_rnn_kernel, tt=tt, cb=cb, pitch=pitch),
        grid=(ncb, t_len // tt),
        in_specs=[
            pl.BlockSpec((b, tt, cb), lambda ci, ti: (0, ti, P_XR * ncb + ci)),
            pl.BlockSpec((b, tt, cb), lambda ci, ti: (0, ti, P_GR * ncb + ci)),
            vec(SUBLANES), vec(1), vec(CONV_W), vec(1),
            pl.BlockSpec((nblk, LRU_BS, 2 * LRU_BS), lambda ci, ti: (ci, 0, 0)),
            vec(1), vec(1), vec(1),
        ],
        out_specs=[
            pl.BlockSpec((b, tt, cb), lambda ci, ti: (0, ti, ci)),
            pl.BlockSpec((b, cb), lambda ci, ti: (0, ci)),
        ],
        out_shape=[
            jax.ShapeDtypeStruct((b, t_len, D_MODEL), BF16),
            jax.ShapeDtypeStruct((b, D_MODEL), F32),
        ],
        scratch_shapes=[
            pltpu.VMEM((b, tt + SUBLANES, cb), F32),
            pltpu.VMEM((nblk, b * pitch, LRU_BS), F32),
            pltpu.VMEM((nblk, b * pitch, LRU_BS), F32),
            pltpu.VMEM((nblk, b * pitch, LRU_BS), F32),
            pltpu.VMEM((b, cb), F32),
        ],
        compiler_params=pltpu.CompilerParams(
            dimension_semantics=("parallel", "arbitrary"),
            vmem_limit_bytes=_vmem_limit(blocks, resident)),
        name="rnn",
    )(z3, z3, ctx, h0, conv_w, conv_b, w_ax, b_a, b_x, lam)


def _forget_kernel(x_ref, fb_ref, nf_ref, *, n_blocks, first_valid):
    lane = lax.broadcasted_iota(jnp.int32, (N_HEADS, LANES), 1)
    fb = fb_ref[...]
    run = jnp.zeros((N_HEADS, 1), F32)
    for k in range(n_blocks):
        x = x_ref[0, :, k * LANES:(k + 1) * LANES] + fb
        lf = jnp.minimum(x, 0.0) - jnp.log1p(jnp.exp(-jnp.abs(x)))
        if k == 0:
            lf = jnp.where(lane >= first_valid, lf, 0.0)
        d = 1
        while d < LANES:
            lf = jnp.where(lane >= d, lf + pltpu.roll(lf, d, axis=1), lf)
            d *= 2
        f = lf + run
        run = f[:, LANES - 1:LANES]
        neg = f * -LOG2E
        if k == 0:
            neg = jnp.where(lane >= first_valid, neg, NEG_BIG)
        nf_ref[0, :, k * LANES:(k + 1) * LANES] = neg


def _forget_prefix(x_all, forget_b):
    b, _, n = x_all.shape
    return pl.pallas_call(
        functools.partial(_forget_kernel, n_blocks=n // LANES, first_valid=LANES - N_META),
        grid=(b,),
        in_specs=[
            pl.BlockSpec((1, N_HEADS, n), lambda bi: (bi, 0, 0)),
            pl.BlockSpec((N_HEADS, 1), lambda bi: (0, 0)),
        ],
        out_specs=pl.BlockSpec((1, N_HEADS, n), lambda bi: (bi, 0, 0)),
        out_shape=jax.ShapeDtypeStruct((b, N_HEADS, n), F32),
        compiler_params=pltpu.CompilerParams(dimension_semantics=("parallel",)),
        name="forget_prefix",
    )(x_all, forget_b)


def _attn_kernel(q_ref, k_ref, v_ref, km_ref, vm_ref, nf_ref, o_ref, *, tq, nq, hg):
    qi = pl.program_id(2)
    causal = (lax.broadcasted_iota(jnp.int32, (tq, tq), 0)
              >= lax.broadcasted_iota(jnp.int32, (tq, tq), 1))

    def head(hh, c):
        cols = slice(hh * HEAD_DIM, (hh + 1) * HEAD_DIM)
        q = q_ref[0, :, cols]

        def scores(k_blk, neg_f):
            s = lax.dot_general(q, k_blk, (((1,), (1,)), ((), ())), preferred_element_type=F32)
            return s + neg_f

        lo = c * tq
        ss = [scores(km_ref[:, cols], nf_ref[hh, :, 0:LANES])]
        vs = [vm_ref[:, cols]]
        if c > 0:
            ss.append(scores(k_ref[0, 0:lo, cols], nf_ref[hh, :, LANES:LANES + lo]))
            vs.append(v_ref[0, 0:lo, cols])
        s_diag = scores(k_ref[0, lo:lo + tq, cols], nf_ref[hh, :, LANES + lo:LANES + lo + tq])
        ss.append(jnp.where(causal, s_diag, NEG_BIG))
        vs.append(v_ref[0, lo:lo + tq, cols])
        m = functools.reduce(jnp.maximum, [jnp.max(s, axis=1, keepdims=True) for s in ss])
        l = jnp.zeros((tq, 1), F32)
        o = jnp.zeros((tq, HEAD_DIM), F32)
        for s, v_blk in zip(ss, vs):
            p = jnp.exp2(s - m)
            l = l + jnp.sum(p, axis=1, keepdims=True)
            o = o + jnp.dot(p.astype(BF16), v_blk, preferred_element_type=F32)
        o_ref[0, :, cols] = (o / l).astype(BF16)

    for c in range(nq):
        @pl.when(qi == c)
        def _(c=c):
            for hh in range(hg):
                head(hh, c)


def _attention(z3, zm_pad, neg_f, *, tq, hg):
    b, t_len, _ = z3.shape
    nq = t_len // tq
    ng = N_HEADS // hg
    w = hg * HEAD_DIM
    blocks = (tq * w * 2 * 2 + 2 * t_len * w * 2 + 2 * LANES * w * 2
              + hg * SUBLANES * (LANES + t_len) * 4)
    resident = hg * 4 * tq * (t_len + LANES) * 4
    return pl.pallas_call(
        functools.partial(_attn_kernel, tq=tq, nq=nq, hg=hg),
        grid=(b, ng, nq),
        in_specs=[
            pl.BlockSpec((1, tq, w), lambda bi, gi, qi: (bi, qi, P_Q * ng + gi)),
            pl.BlockSpec((1, t_len, w), lambda bi, gi, qi: (bi, 0, P_K * ng + gi)),
            pl.BlockSpec((1, t_len, w), lambda bi, gi, qi: (bi, 0, P_V * ng + gi)),
            pl.BlockSpec((LANES, w), lambda bi, gi, qi: (0, P_K * ng + gi)),
            pl.BlockSpec((LANES, w), lambda bi, gi, qi: (0, P_V * ng + gi)),
            pl.BlockSpec((hg, 1, LANES + t_len), lambda bi, gi, qi: (bi * ng + gi, 0, 0)),
        ],
        out_specs=pl.BlockSpec((1, tq, w), lambda bi, gi, qi: (bi, qi, gi)),
        out_shape=jax.ShapeDtypeStruct((b, t_len, N_HEADS * HEAD_DIM), BF16),
        compiler_params=pltpu.CompilerParams(
            dimension_semantics=("parallel", "parallel", "arbitrary"),
            vmem_limit_bytes=_vmem_limit(blocks, resident)),
        name="attention",
    )(z3, z3, z3, zm_pad, zm_pad, neg_f)


def _merge_kernel(yr_ref, ya_ref, wr_ref, wa_ref, gr_ref, ga_ref, o_ref):
    pr = jnp.dot(yr_ref[...], wr_ref[...], preferred_element_type=F32)
    pa = jnp.dot(ya_ref[...], wa_ref[...], preferred_element_type=F32)
    merged = (jax.nn.sigmoid(gr_ref[...].astype(F32)) * pr
              + jax.nn.sigmoid(ga_ref[...].astype(F32)) * pa)
    o_ref[...] = merged.astype(BF16)


def _merge(y_rnn, y_attn, w_rnn_out, w_attn_out, zg, *, tm, tn):
    m = y_rnn.shape[0]
    nn = D_MODEL // tn
    blocks = 2 * tm * D_MODEL * 2 + 2 * D_MODEL * tn * 2 + 3 * tm * tn * 2
    resident = 3 * tm * tn * 4
    return pl.pallas_call(
        _merge_kernel,
        grid=(m // tm, nn),
        in_specs=[
            pl.BlockSpec((tm, D_MODEL), lambda i, j: (i, 0)),
            pl.BlockSpec((tm, D_MODEL), lambda i, j: (i, 0)),
            pl.BlockSpec((D_MODEL, tn), lambda i, j: (0, j)),
            pl.BlockSpec((D_MODEL, tn), lambda i, j: (0, j)),
            pl.BlockSpec((tm, tn), lambda i, j: (i, P_GRNN * nn + j)),
            pl.BlockSpec((tm, tn), lambda i, j: (i, P_GATTN * nn + j)),
        ],
        out_specs=pl.BlockSpec((tm, tn), lambda i, j: (i, j)),
        out_shape=jax.ShapeDtypeStruct((m, D_MODEL), BF16),
        compiler_params=pltpu.CompilerParams(
            dimension_semantics=("parallel", "arbitrary"),
            vmem_limit_bytes=_vmem_limit(blocks, resident)),
        name="merge",
    )(y_rnn, y_attn, w_rnn_out, w_attn_out, zg, zg)


def _out_proj_kernel(x_ref, w_ref, h_ref, g_ref, o_ref):
    mix = jnp.dot(x_ref[...], w_ref[...], preferred_element_type=F32)
    o_ref[...] = h_ref[...] + _rms(mix, g_ref[...])


def _out_proj(merged, w_o, h, g_post, *, tm):
    m = merged.shape[0]
    blocks = tm * D_MODEL * (2 + 4 + 4) + D_MODEL * D_MODEL * 2
    resident = tm * D_MODEL * 4
    return pl.pallas_call(
        _out_proj_kernel,
        grid=(m // tm,),
        in_specs=[
            pl.BlockSpec((tm, D_MODEL), lambda i: (i, 0)),
            pl.BlockSpec((D_MODEL, D_MODEL), lambda i: (0, 0)),
            pl.BlockSpec((tm, D_MODEL), lambda i: (i, 0)),
            pl.BlockSpec((1, D_MODEL), lambda i: (0, 0)),
        ],
        out_specs=pl.BlockSpec((tm, D_MODEL), lambda i: (i, 0)),
        out_shape=jax.ShapeDtypeStruct((m, D_MODEL), F32),
        compiler_params=pltpu.CompilerParams(
            dimension_semantics=("parallel",),
            vmem_limit_bytes=_vmem_limit(blocks, resident)),
        name="out_proj",
    )(merged, w_o, h, g_post)


def kernel(x, meta_tokens, norm_g, ffn1_w_gu, ffn1_w_down, w_in, conv_w, conv_b, lru_w_a, lru_b_a,
           lru_w_x, lru_b_x, lru_lambda, forget_b, w_rnn_out, w_attn_out, w_o, ffn2_w_gu,
           ffn2_w_down):
    b, t_len, d = x.shape
    m = b * t_len
    g = [norm_g[0, k:k + 1] for k in range(6)]
    row = lambda v: v[0].reshape(1, -1)

    w_in_t = w_in[0].T.astype(BF16)
    w_fl_t = w_in_t[Z_COLS:Z_COLS + LANES]
    w_hi_t = w_in_t[Z_COLS + N_HEADS:]
    w_ax = jnp.concatenate([lru_w_a[0], lru_w_x[0]], axis=-1).astype(BF16)
    w_gu1, w_d1 = ffn1_w_gu[0].astype(BF16), ffn1_w_down[0].astype(BF16)
    w_gu2, w_d2 = ffn2_w_gu[0].astype(BF16), ffn2_w_down[0].astype(BF16)
    w_r, w_a, w_out = w_rnn_out[0].astype(BF16), w_attn_out[0].astype(BF16), w_o[0].astype(BF16)
    rnn_params = (conv_w[0], row(conv_b), w_ax, row(lru_b_a), row(lru_b_x), row(lru_lambda))

    _, u_m = _ffn(meta_tokens, g[0], w_gu1, w_d1, g[1], g[2], tm=N_META, h_buffers=2)
    z_m = _in_proj(u_m, w_in_t, tm=N_META, tn=1024)
    _, fl_m = _gate_proj(u_m, w_hi_t, w_fl_t, tm=N_META, tn=1024)
    zeros = jnp.zeros((SUBLANES, D_MODEL), F32)
    _, h_meta = _rnn(jnp.broadcast_to(z_m[None], (SUBLANES, N_META, Z_COLS)), zeros, zeros[:1],
                     *rnn_params, tt=N_META, cb=512)
    ctx = z_m[N_META - SUBLANES:, :D_MODEL].astype(F32)
    zm_pad = jnp.pad(z_m, ((LANES - N_META, 0), (0, 0)))

    h1, u2 = _ffn(x.reshape(m, d), g[0], w_gu1, w_d1, g[1], g[2], tm=512, h_buffers=2)
    z = _in_proj(u2, w_in_t, tm=1024, tn=1024)
    zg, fl = _gate_proj(u2, w_hi_t, w_fl_t, tm=1024, tn=1024)
    z3 = z.reshape(b, t_len, Z_COLS)
    y_rnn, _ = _rnn(z3, ctx, h_meta[:1], *rnn_params, tt=256, cb=512)

    fl_meta = jnp.pad(fl_m[:, :N_HEADS].T, ((0, 0), (LANES - N_META, 0)))
    fl_real = fl[:, :N_HEADS].reshape(b, t_len, N_HEADS).transpose(0, 2, 1)
    x_all = jnp.concatenate([jnp.broadcast_to(fl_meta[None], (b, N_HEADS, LANES)), fl_real], axis=2)
    neg_f = _forget_prefix(x_all, forget_b[0].reshape(N_HEADS, 1))
    y_attn = _attention(z3, zm_pad, neg_f.reshape(b * N_HEADS, 1, LANES + t_len), tq=512, hg=4)

    merged = _merge(y_rnn.reshape(m, d), y_attn.reshape(m, d), w_r, w_a, zg, tm=1024, tn=512)
    h2 = _out_proj(merged, w_out, h1, g[3], tm=512)
    (h3,) = _ffn(h2, g[4], w_gu2, w_d2, g[5], None, tm=512, h_buffers=2)
    return h3.reshape(b, t_len, d)
```

```python
import functools

import jax
import jax.numpy as jnp
from jax import lax
from jax.experimental import pallas as pl
from jax.experimental.pallas import tpu as pltpu

D_MODEL = 2048
N_META = 16
D_FF = 5632
N_HEADS = 16
HEAD_DIM = 128
LRU_BS = 128
LRU_BLOCKS = D_MODEL // LRU_BS
CONV_W = 4
LRU_C = 8.0
EPS = 1e-6
(P_XR, P_GR, P_Q, P_K, P_V) = range(5)
Z_COLS = 5 * D_MODEL
(P_GRNN, P_GATTN) = range(2)
ZG_COLS = 2 * D_MODEL
FFN_TF = 512

LANES = 128
SUBLANES = 8
VMEM_BYTES_V7X = 64 * 1024 * 1024
NEG_BIG = -1e30
LOG2E = 1.4426950408889634
QK_SCALE_LOG2 = HEAD_DIM ** -0.5 * LOG2E

F32 = jnp.float32
BF16 = jnp.bfloat16


def _vmem_limit(pipelined_bytes, resident_bytes):
    return min(2 * pipelined_bytes + resident_bytes + (4 << 20), VMEM_BYTES_V7X - (4 << 20))


def _rms(x, g):
    return (x * lax.rsqrt(jnp.mean(x * x, axis=-1, keepdims=True) + EPS)) * g


def _ffn_kernel(h_ref, gpre_ref, wg_ref, wu_ref, wd_ref, gpost_ref, *rest, with_next):
    if with_next:
        gnext_ref, out_ref, unext_ref, u_scr = rest
    else:
        out_ref, u_scr = rest
    j = pl.program_id(1)

    @pl.when(j == 0)
    def _():
        u_scr[...] = _rms(h_ref[...], gpre_ref[...]).astype(BF16)
        out_ref[...] = jnp.zeros_like(out_ref)

    u = u_scr[...]
    gate = jnp.dot(u, wg_ref[...], preferred_element_type=F32)
    up = jnp.dot(u, wu_ref[...], preferred_element_type=F32)
    act = ((gate * jax.nn.sigmoid(gate)) * up).astype(BF16)
    out_ref[...] += jnp.dot(act, wd_ref[...], preferred_element_type=F32)

    @pl.when(j == pl.num_programs(1) - 1)
    def _():
        h_new = h_ref[...] + 0.5 * _rms(out_ref[...], gpost_ref[...])
        out_ref[...] = h_new
        if with_next:
            unext_ref[...] = _rms(h_new, gnext_ref[...]).astype(BF16)


def _ffn(h, g_pre, w_gu, w_down, g_post, g_next, *, tm, h_buffers):
    m = h.shape[0]
    tf = FFN_TF
    nf = D_FF // tf
    with_next = g_next is not None
    row_tile = pl.BlockSpec((tm, D_MODEL), lambda i, j: (i, 0))
    gain = pl.BlockSpec((1, D_MODEL), lambda i, j: (0, 0))
    blocks = tm * D_MODEL * (4 + 2 * with_next) + 3 * D_MODEL * tf * 2
    resident = tm * D_MODEL * (4 * h_buffers + 2 + 4) + tm * tf * (4 + 4 + 2)
    return pl.pallas_call(
        functools.partial(_ffn_kernel, with_next=with_next),
        grid=(m // tm, nf),
        in_specs=[
            pl.BlockSpec((tm, D_MODEL), lambda i, j: (i, 0), pipeline_mode=pl.Buffered(h_buffers)),
            gain,
            pl.BlockSpec((D_MODEL, tf), lambda i, j: (0, j)),
            pl.BlockSpec((D_MODEL, tf), lambda i, j: (0, j + nf)),
            pl.BlockSpec((tf, D_MODEL), lambda i, j: (j, 0)),
            gain,
        ] + [gain] * with_next,
        out_specs=[row_tile] * (1 + with_next),
        out_shape=[jax.ShapeDtypeStruct((m, D_MODEL), F32)]
        + [jax.ShapeDtypeStruct((m, D_MODEL), BF16)] * with_next,
        scratch_shapes=[pltpu.VMEM((tm, D_MODEL), BF16)],
        compiler_params=pltpu.CompilerParams(
            dimension_semantics=("parallel", "arbitrary"),
            vmem_limit_bytes=_vmem_limit(blocks, resident)),
        name="ffn",
    )(h, g_pre, w_gu, w_gu, w_down, g_post, *([g_next] * with_next))


def _dot_nt(x, wt):
    return lax.dot_general(x, wt, (((1,), (1,)), ((), ())), preferred_element_type=F32)


def _in_proj_kernel(u_ref, wt_ref, z_ref, *, tn):
    j = pl.program_id(0)
    is_q = (j >= P_Q * D_MODEL // tn) & (j < (P_Q + 1) * D_MODEL // tn)
    factor = jnp.where(is_q, QK_SCALE_LOG2, 1.0)
    z_ref[...] = (_dot_nt(u_ref[...], wt_ref[...]) * factor).astype(BF16)


def _in_proj(u, w_in_t, *, tm, tn):
    m = u.shape[0]
    blocks = tm * D_MODEL * 2 + D_MODEL * tn * 2 + tm * tn * 2
    resident = tm * tn * 4
    return pl.pallas_call(
        functools.partial(_in_proj_kernel, tn=tn),
        grid=(Z_COLS // tn, m // tm),
        in_specs=[
            pl.BlockSpec((tm, D_MODEL), lambda j, i: (i, 0)),
            pl.BlockSpec((tn, D_MODEL), lambda j, i: (j, 0)),
        ],
        out_specs=pl.BlockSpec((tm, tn), lambda j, i: (i, j)),
        out_shape=jax.ShapeDtypeStruct((m, Z_COLS), BF16),
        compiler_params=pltpu.CompilerParams(
            dimension_semantics=("parallel", "parallel"),
            vmem_limit_bytes=_vmem_limit(blocks, resident)),
        name="in_proj",
    )(u, w_in_t)


def _gate_proj_kernel(u_ref, wt_ref, wflt_ref, zg_ref, fl_ref):
    u = u_ref[...]
    zg_ref[...] = _dot_nt(u, wt_ref[...]).astype(BF16)

    @pl.when(pl.program_id(1) == 0)
    def _():
        fl_ref[...] = _dot_nt(u, wflt_ref[...])


def _gate_proj(u, w_hi_t, w_fl_t, *, tm, tn):
    m = u.shape[0]
    blocks = tm * D_MODEL * 2 + D_MODEL * tn * 2 + D_MODEL * LANES * 2 + tm * tn * 2 + tm * LANES * 4
    resident = tm * tn * 4
    return pl.pallas_call(
        _gate_proj_kernel,
        grid=(m // tm, ZG_COLS // tn),
        in_specs=[
            pl.BlockSpec((tm, D_MODEL), lambda i, j: (i, 0)),
            pl.BlockSpec((tn, D_MODEL), lambda i, j: (j, 0)),
            pl.BlockSpec((LANES, D_MODEL), lambda i, j: (0, 0)),
        ],
        out_specs=[
            pl.BlockSpec((tm, tn), lambda i, j: (i, j)),
            pl.BlockSpec((tm, LANES), lambda i, j: (i, 0)),
        ],
        out_shape=[
            jax.ShapeDtypeStruct((m, ZG_COLS), BF16),
            jax.ShapeDtypeStruct((m, LANES), F32),
        ],
        compiler_params=pltpu.CompilerParams(
            dimension_semantics=("parallel", "arbitrary"),
            vmem_limit_bytes=_vmem_limit(blocks, resident)),
        name="gate_proj",
    )(u, w_hi_t, w_fl_t)


def _rnn_kernel(xr_ref, gr_ref, ctx_ref, h0_ref, cw_ref, cb_ref, wax_ref, ba_ref, bx_ref, lam_ref,
                y_ref, hlast_ref, xbuf, a_scr, b_scr, h_scr, carry_scr, *, tt, cb, pitch):
    t = pl.program_id(1)
    nslab = cb // LRU_BS

    @pl.when(t == 0)
    def _():
        xbuf[:, 0:SUBLANES, :] = jnp.broadcast_to(ctx_ref[...][None], (SUBLANES, SUBLANES, cb))
        carry_scr[...] = jnp.broadcast_to(h0_ref[...], (SUBLANES, cb))

    x = xr_ref[...].astype(F32)
    xbuf[:, SUBLANES:SUBLANES + tt, :] = x
    xc = xbuf[:, pl.ds(SUBLANES - 3, tt), :] * cw_ref[0:1, :]
    xc = xc + xbuf[:, pl.ds(SUBLANES - 2, tt), :] * cw_ref[1:2, :]
    xc = xc + xbuf[:, pl.ds(SUBLANES - 1, tt), :] * cw_ref[2:3, :]
    xc = xc + x * cw_ref[3:4, :]
    xc = xc + cb_ref[...]
    xbuf[:, 0:SUBLANES, :] = xbuf[:, tt:tt + SUBLANES, :]

    for n in range(nslab):
        cols = slice(n * LRU_BS, (n + 1) * LRU_BS)
        xn = xc[:, :, cols].reshape(SUBLANES * tt, LRU_BS)
        g = jnp.dot(xn.astype(BF16), wax_ref[n], preferred_element_type=F32)
        r = jax.nn.sigmoid(g[:, :LRU_BS] + ba_ref[:, cols])
        i = jax.nn.sigmoid(g[:, LRU_BS:] + bx_ref[:, cols])
        neg_lam = -lam_ref[:, cols]
        softplus = jnp.maximum(neg_lam, 0.0) + jnp.log1p(jnp.exp(-jnp.abs(neg_lam)))
        a = jnp.exp2((-LRU_C * LOG2E * softplus) * r)
        bv = jnp.sqrt(1.0 - a * a) * (i * xn)
        for b in range(SUBLANES):
            a_scr[n, b * pitch:b * pitch + tt, :] = a[b * tt:(b + 1) * tt]
            b_scr[n, b * pitch:b * pitch + tt, :] = bv[b * tt:(b + 1) * tt]

    def step(tl, hs):
        rows = pl.ds(tl, SUBLANES, stride=pitch)
        hs = tuple(a_scr[n, rows, :] * hs[n] + b_scr[n, rows, :] for n in range(nslab))
        for n in range(nslab):
            h_scr[n, rows, :] = hs[n]
        return hs

    hs = tuple(carry_scr[:, n * LRU_BS:(n + 1) * LRU_BS] for n in range(nslab))
    hs = lax.fori_loop(0, tt, step, hs, unroll=8)
    for n in range(nslab):
        cols = slice(n * LRU_BS, (n + 1) * LRU_BS)
        carry_scr[:, cols] = hs[n]
        hlast_ref[:, cols] = hs[n]
        for b in range(SUBLANES):
            gate = jax.nn.gelu(gr_ref[b, :, cols].astype(F32), approximate=True)
            y_ref[b, :, cols] = (h_scr[n, b * pitch:b * pitch + tt, :] * gate).astype(BF16)


def _rnn(z3, ctx, h0, conv_w, conv_b, w_ax, b_a, b_x, lam, *, tt, cb):
    b, t_len, _ = z3.shape
    assert b == SUBLANES
    ncb = D_MODEL // cb
    nblk = cb // LRU_BS
    pitch = tt + SUBLANES if (tt // SUBLANES) % 2 == 0 else tt + 2 * SUBLANES
    vec = lambda rows: pl.BlockSpec((rows, cb), lambda ci, ti: (0, ci))
    blocks = 3 * b * tt * cb * 2 + nblk * LRU_BS * 2 * LRU_BS * 2 + b * cb * 4
    resident = b * (tt + SUBLANES) * cb * 4 + 3 * b * pitch * cb * 4 + 6 * b * tt * cb * 4
    return pl.pallas_call(
        functools.partial(_rnn_kernel, tt=tt, cb=cb, pitch=pitch),
        grid=(ncb, t_len // tt),
        in_specs=[
            pl.BlockSpec((b, tt, cb), lambda ci, ti: (0, ti, P_XR * ncb + ci)),
            pl.BlockSpec((b, tt, cb), lambda ci, ti: (0, ti, P_GR * ncb + ci)),
            vec(SUBLANES), vec(1), vec(CONV_W), vec(1),
            pl.BlockSpec((nblk, LRU_BS, 2 * LRU_BS), lambda ci, ti: (ci, 0, 0)),
            vec(1), vec(1), vec(1),
        ],
        out_specs=[
            pl.BlockSpec((b, tt, cb), lambda ci, ti: (0, ti, ci)),
            pl.BlockSpec((b, cb), lambda ci, ti: (0, ci)),
        ],
        out_shape=[
            jax.ShapeDtypeStruct((b, t_len, D_MODEL), BF16),
            jax.ShapeDtypeStruct((b, D_MODEL), F32),
        ],
        scratch_shapes=[
            pltpu.VMEM((b, tt + SUBLANES, cb), F32),
            pltpu.VMEM((nblk, b * pitch, LRU_BS), F32),
            pltpu.VMEM((nblk, b * pitch, LRU_BS), F32),
            pltpu.VMEM((nblk, b * pitch, LRU_BS), F32),
            pltpu.VMEM((b, cb), F32),
        ],
        compiler_params=pltpu.CompilerParams(
            dimension_semantics=("parallel", "arbitrary"),
            vmem_limit_bytes=_vmem_limit(blocks, resident)),
        name="rnn",
    )(z3, z3, ctx, h0, conv_w, conv_b, w_ax, b_a, b_x, lam)


def _forget_kernel(x_ref, fb_ref, nf_ref, *, n_blocks, first_valid):
    lane = lax.broadcasted_iota(jnp.int32, (N_HEADS, LANES), 1)
    fb = fb_ref[...]
    run = jnp.zeros((N_HEADS, 1), F32)
    for k in range(n_blocks):
        x = x_ref[0, :, k * LANES:(k + 1) * LANES] + fb
        lf = jnp.minimum(x, 0.0) - jnp.log1p(jnp.exp(-jnp.abs(x)))
        if k == 0:
            lf = jnp.where(lane >= first_valid, lf, 0.0)
        d = 1
        while d < LANES:
            lf = jnp.where(lane >= d, lf + pltpu.roll(lf, d, axis=1), lf)
            d *= 2
        f = lf + run
        run = f[:, LANES - 1:LANES]
        neg = f * -LOG2E
        if k == 0:
            neg = jnp.where(lane >= first_valid, neg, NEG_BIG)
        nf_ref[0, :, k * LANES:(k + 1) * LANES] = neg


def _forget_prefix(x_all, forget_b):
    b, _, n = x_all.shape
    return pl.pallas_call(
        functools.partial(_forget_kernel, n_blocks=n // LANES, first_valid=LANES - N_META),
        grid=(b,),
        in_specs=[
            pl.BlockSpec((1, N_HEADS, n), lambda bi: (bi, 0, 0)),
            pl.BlockSpec((N_HEADS, 1), lambda bi: (0, 0)),
        ],
        out_specs=pl.BlockSpec((1, N_HEADS, n), lambda bi: (bi, 0, 0)),
        out_shape=jax.ShapeDtypeStruct((b, N_HEADS, n), F32),
        compiler_params=pltpu.CompilerParams(dimension_semantics=("parallel",)),
        name="forget_prefix",
    )(x_all, forget_b)


def _attn_kernel(q_ref, k_ref, v_ref, km_ref, vm_ref, nf_ref, o_ref, vx_scr, vmx_scr, *, tq, nq, hg):
    qi = pl.program_id(2)
    causal = (lax.broadcasted_iota(jnp.int32, (tq, tq), 0)
              >= lax.broadcasted_iota(jnp.int32, (tq, tq), 1))

    @pl.when(qi == 0)
    def _():
        for hh in range(hg):
            cols = slice(hh * HEAD_DIM, (hh + 1) * HEAD_DIM)
            vx_scr[hh, :, 0:HEAD_DIM] = v_ref[0, :, cols]
            vx_scr[hh, :, HEAD_DIM:] = jnp.ones((vx_scr.shape[1], HEAD_DIM), BF16)
            vmx_scr[hh, :, 0:HEAD_DIM] = vm_ref[:, cols]
            vmx_scr[hh, :, HEAD_DIM:] = jnp.ones((LANES, HEAD_DIM), BF16)

    def head(hh, c):
        cols = slice(hh * HEAD_DIM, (hh + 1) * HEAD_DIM)
        q = q_ref[0, :, cols]

        def scores(k_blk, neg_f):
            s = lax.dot_general(q, k_blk, (((1,), (1,)), ((), ())), preferred_element_type=F32)
            return s + neg_f

        lo = c * tq
        ss = [scores(km_ref[:, cols], nf_ref[hh, :, 0:LANES])]
        vs = [vmx_scr[hh]]
        if c > 0:
            ss.append(scores(k_ref[0, 0:lo, cols], nf_ref[hh, :, LANES:LANES + lo]))
            vs.append(vx_scr[hh, 0:lo, :])
        s_diag = scores(k_ref[0, lo:lo + tq, cols], nf_ref[hh, :, LANES + lo:LANES + lo + tq])
        ss.append(jnp.where(causal, s_diag, NEG_BIG))
        vs.append(vx_scr[hh, lo:lo + tq, :])
        m = functools.reduce(jnp.maximum, [jnp.max(s, axis=1, keepdims=True) for s in ss])
        ol = jnp.zeros((tq, 2 * HEAD_DIM), F32)
        for s, v_blk in zip(ss, vs):
            ol = ol + jnp.dot(jnp.exp2(s - m).astype(BF16), v_blk, preferred_element_type=F32)
        o_ref[0, :, cols] = (ol[:, :HEAD_DIM] / ol[:, HEAD_DIM:]).astype(BF16)

    for c in range(nq):
        @pl.when(qi == c)
        def _(c=c):
            for hh in range(hg):
                head(hh, c)


def _attention(z3, zm_pad, neg_f, *, tq, hg):
    b, t_len, _ = z3.shape
    nq = t_len // tq
    ng = N_HEADS // hg
    w = hg * HEAD_DIM
    blocks = (tq * w * 2 * 2 + 2 * t_len * w * 2 + 2 * LANES * w * 2
              + hg * SUBLANES * (LANES + t_len) * 4)
    resident = hg * 4 * tq * (t_len + LANES) * 4 + hg * (t_len + LANES) * 2 * HEAD_DIM * 2
    return pl.pallas_call(
        functools.partial(_attn_kernel, tq=tq, nq=nq, hg=hg),
        grid=(b, ng, nq),
        in_specs=[
            pl.BlockSpec((1, tq, w), lambda bi, gi, qi: (bi, qi, P_Q * ng + gi)),
            pl.BlockSpec((1, t_len, w), lambda bi, gi, qi: (bi, 0, P_K * ng + gi)),
            pl.BlockSpec((1, t_len, w), lambda bi, gi, qi: (bi, 0, P_V * ng + gi)),
            pl.BlockSpec((LANES, w), lambda bi, gi, qi: (0, P_K * ng + gi)),
            pl.BlockSpec((LANES, w), lambda bi, gi, qi: (0, P_V * ng + gi)),
            pl.BlockSpec((hg, 1, LANES + t_len), lambda bi, gi, qi: (bi * ng + gi, 0, 0)),
        ],
        out_specs=pl.BlockSpec((1, tq, w), lambda bi, gi, qi: (bi, qi, gi)),
        out_shape=jax.ShapeDtypeStruct((b, t_len, N_HEADS * HEAD_DIM), BF16),
        scratch_shapes=[
            pltpu.VMEM((hg, t_len, 2 * HEAD_DIM), BF16),
            pltpu.VMEM((hg, LANES, 2 * HEAD_DIM), BF16),
        ],
        compiler_params=pltpu.CompilerParams(
            dimension_semantics=("parallel", "parallel", "arbitrary"),
            vmem_limit_bytes=_vmem_limit(blocks, resident)),
        name="attention",
    )(z3, z3, z3, zm_pad, zm_pad, neg_f)


def _merge_kernel(yr_ref, ya_ref, wr_ref, wa_ref, gr_ref, ga_ref, o_ref):
    pr = jnp.dot(yr_ref[...], wr_ref[...], preferred_element_type=F32)
    pa = jnp.dot(ya_ref[...], wa_ref[...], preferred_element_type=F32)
    merged = (jax.nn.sigmoid(gr_ref[...].astype(F32)) * pr
              + jax.nn.sigmoid(ga_ref[...].astype(F32)) * pa)
    o_ref[...] = merged.astype(BF16)


def _merge(y_rnn, y_attn, w_rnn_out, w_attn_out, zg, *, tm, tn):
    m = y_rnn.shape[0]
    nn = D_MODEL // tn
    blocks = 2 * tm * D_MODEL * 2 + 2 * D_MODEL * tn * 2 + 3 * tm * tn * 2
    resident = 3 * tm * tn * 4
    return pl.pallas_call(
        _merge_kernel,
        grid=(m // tm, nn),
        in_specs=[
            pl.BlockSpec((tm, D_MODEL), lambda i, j: (i, 0)),
            pl.BlockSpec((tm, D_MODEL), lambda i, j: (i, 0)),
            pl.BlockSpec((D_MODEL, tn), lambda i, j: (0, j)),
            pl.BlockSpec((D_MODEL, tn), lambda i, j: (0, j)),
            pl.BlockSpec((tm, tn), lambda i, j: (i, P_GRNN * nn + j)),
            pl.BlockSpec((tm, tn), lambda i, j: (i, P_GATTN * nn + j)),
        ],
        out_specs=pl.BlockSpec((tm, tn), lambda i, j: (i, j)),
        out_shape=jax.ShapeDtypeStruct((m, D_MODEL), BF16),
        compiler_params=pltpu.CompilerParams(
            dimension_semantics=("parallel", "arbitrary"),
            vmem_limit_bytes=_vmem_limit(blocks, resident)),
        name="merge",
    )(y_rnn, y_attn, w_rnn_out, w_attn_out, zg, zg)


def _out_proj_kernel(x_ref, w_ref, h_ref, g_ref, o_ref):
    mix = jnp.dot(x_ref[...], w_ref[...], preferred_element_type=F32)
    o_ref[...] = h_ref[...] + _rms(mix, g_ref[...])


def _out_proj(merged, w_o, h, g_post, *, tm):
    m = merged.shape[0]
    blocks = tm * D_MODEL * (2 + 4 + 4) + D_MODEL * D_MODEL * 2
    resident = tm * D_MODEL * 4
    return pl.pallas_call(
        _out_proj_kernel,
        grid=(m // tm,),
        in_specs=[
            pl.BlockSpec((tm, D_MODEL), lambda i: (i, 0)),
            pl.BlockSpec((D_MODEL, D_MODEL), lambda i: (0, 0)),
            pl.BlockSpec((tm, D_MODEL), lambda i: (i, 0)),
            pl.BlockSpec((1, D_MODEL), lambda i: (0, 0)),
        ],
        out_specs=pl.BlockSpec((tm, D_MODEL), lambda i: (i, 0)),
        out_shape=jax.ShapeDtypeStruct((m, D_MODEL), F32),
        compiler_params=pltpu.CompilerParams(
            dimension_semantics=("parallel",),
            vmem_limit_bytes=_vmem_limit(blocks, resident)),
        name="out_proj",
    )(merged, w_o, h, g_post)


def kernel(x, meta_tokens, norm_g, ffn1_w_gu, ffn1_w_down, w_in, conv_w, conv_b, lru_w_a, lru_b_a,
           lru_w_x, lru_b_x, lru_lambda, forget_b, w_rnn_out, w_attn_out, w_o, ffn2_w_gu,
           ffn2_w_down):
    b, t_len, d = x.shape
    m = b * t_len
    g = [norm_g[0, k:k + 1] for k in range(6)]
    row = lambda v: v[0].reshape(1, -1)

    w_in_t = w_in[0].T.astype(BF16)
    w_fl_t = w_in_t[Z_COLS:Z_COLS + LANES]
    w_hi_t = w_in_t[Z_COLS + N_HEADS:]
    w_ax = jnp.concatenate([lru_w_a[0], lru_w_x[0]], axis=-1).astype(BF16)
    w_gu1, w_d1 = ffn1_w_gu[0].astype(BF16), ffn1_w_down[0].astype(BF16)
    w_gu2, w_d2 = ffn2_w_gu[0].astype(BF16), ffn2_w_down[0].astype(BF16)
    w_r, w_a, w_out = w_rnn_out[0].astype(BF16), w_attn_out[0].astype(BF16), w_o[0].astype(BF16)
    rnn_params = (conv_w[0], row(conv_b), w_ax, row(lru_b_a), row(lru_b_x), row(lru_lambda))

    _, u_m = _ffn(meta_tokens, g[0], w_gu1, w_d1, g[1], g[2], tm=N_META, h_buffers=2)
    z_m = _in_proj(u_m, w_in_t, tm=N_META, tn=1024)
    _, fl_m = _gate_proj(u_m, w_hi_t, w_fl_t, tm=N_META, tn=1024)
    zeros = jnp.zeros((SUBLANES, D_MODEL), F32)
    _, h_meta = _rnn(jnp.broadcast_to(z_m[None], (SUBLANES, N_META, Z_COLS)), zeros, zeros[:1],
                     *rnn_params, tt=N_META, cb=512)
    ctx = z_m[N_META - SUBLANES:, :D_MODEL].astype(F32)
    zm_pad = jnp.pad(z_m, ((LANES - N_META, 0), (0, 0)))

    h1, u2 = _ffn(x.reshape(m, d), g[0], w_gu1, w_d1, g[1], g[2], tm=512, h_buffers=2)
    z = _in_proj(u2, w_in_t, tm=1024, tn=1024)
    zg, fl = _gate_proj(u2, w_hi_t, w_fl_t, tm=1024, tn=1024)
    z3 = z.reshape(b, t_len, Z_COLS)
    y_rnn, _ = _rnn(z3, ctx, h_meta[:1], *rnn_params, tt=256, cb=512)

    fl_meta = jnp.pad(fl_m[:, :N_HEADS].T, ((0, 0), (LANES - N_META, 0)))
    fl_real = fl[:, :N_HEADS].reshape(b, t_len, N_HEADS).transpose(0, 2, 1)
    x_all = jnp.concatenate([jnp.broadcast_to(fl_meta[None], (b, N_HEADS, LANES)), fl_real], axis=2)
    neg_f = _forget_prefix(x_all, forget_b[0].reshape(N_HEADS, 1))
    y_attn = _attention(z3, zm_pad, neg_f.reshape(b * N_HEADS, 1, LANES + t_len), tq=512, hg=4)

    merged = _merge(y_rnn.reshape(m, d), y_attn.reshape(m, d), w_r, w_a, zg, tm=1024, tn=512)
    h2 = _out_proj(merged, w_out, h1, g[3], tm=512)
    (h3,) = _ffn(h2, g[4], w_gu2, w_d2, g[5], None, tm=512, h_buffers=2)
    return h3.reshape(b, t_len, d)
```

```python
import functools

import jax
import jax.numpy as jnp
from jax import lax
from jax.experimental import pallas as pl
from jax.experimental.pallas import tpu as pltpu

D_MODEL = 2048
N_META = 16
D_FF = 5632
N_HEADS = 16
HEAD_DIM = 128
LRU_BS = 128
LRU_BLOCKS = D_MODEL // LRU_BS
CONV_W = 4
LRU_C = 8.0
EPS = 1e-6
(P_XR, P_GR, P_Q, P_K, P_V) = range(5)
Z_COLS = 5 * D_MODEL
(P_GRNN, P_GATTN) = range(2)
ZG_COLS = 2 * D_MODEL
FFN_TF = 512

LANES = 128
SUBLANES = 8
VMEM_BYTES_V7X = 64 * 1024 * 1024
NEG_BIG = -1e30
LOG2E = 1.4426950408889634
QK_SCALE_LOG2 = HEAD_DIM ** -0.5 * LOG2E

F32 = jnp.float32
BF16 = jnp.bfloat16


def _vmem_limit(pipelined_bytes, resident_bytes):
    return min(2 * pipelined_bytes + resident_bytes + (4 << 20), VMEM_BYTES_V7X - (4 << 20))


def _rms(x, g):
    return (x * lax.rsqrt(jnp.mean(x * x, axis=-1, keepdims=True) + EPS)) * g


def _ffn_kernel(h_ref, gpre_ref, wg_ref, wu_ref, wd_ref, gpost_ref, *rest, with_next):
    if with_next:
        gnext_ref, out_ref, unext_ref, u_scr = rest
    else:
        out_ref, u_scr = rest
    j = pl.program_id(1)

    @pl.when(j == 0)
    def _():
        u_scr[...] = _rms(h_ref[...], gpre_ref[...]).astype(BF16)
        out_ref[...] = jnp.zeros_like(out_ref)

    u = u_scr[...]
    gate = jnp.dot(u, wg_ref[...], preferred_element_type=F32)
    up = jnp.dot(u, wu_ref[...], preferred_element_type=F32)
    act = ((gate * jax.nn.sigmoid(gate)) * up).astype(BF16)
    out_ref[...] += jnp.dot(act, wd_ref[...], preferred_element_type=F32)

    @pl.when(j == pl.num_programs(1) - 1)
    def _():
        h_new = h_ref[...] + 0.5 * _rms(out_ref[...], gpost_ref[...])
        out_ref[...] = h_new
        if with_next:
            unext_ref[...] = _rms(h_new, gnext_ref[...]).astype(BF16)


def _ffn(h, g_pre, w_gu, w_down, g_post, g_next, *, tm, h_buffers):
    m = h.shape[0]
    tf = FFN_TF
    nf = D_FF // tf
    with_next = g_next is not None
    row_tile = pl.BlockSpec((tm, D_MODEL), lambda i, j: (i, 0))
    gain = pl.BlockSpec((1, D_MODEL), lambda i, j: (0, 0))
    blocks = tm * D_MODEL * (4 + 2 * with_next) + 3 * D_MODEL * tf * 2
    resident = tm * D_MODEL * (4 * h_buffers + 2 + 4) + tm * tf * (4 + 4 + 2)
    return pl.pallas_call(
        functools.partial(_ffn_kernel, with_next=with_next),
        grid=(m // tm, nf),
        in_specs=[
            pl.BlockSpec((tm, D_MODEL), lambda i, j: (i, 0), pipeline_mode=pl.Buffered(h_buffers)),
            gain,
            pl.BlockSpec((D_MODEL, tf), lambda i, j: (0, j)),
            pl.BlockSpec((D_MODEL, tf), lambda i, j: (0, j + nf)),
            pl.BlockSpec((tf, D_MODEL), lambda i, j: (j, 0)),
            gain,
        ] + [gain] * with_next,
        out_specs=[row_tile] * (1 + with_next),
        out_shape=[jax.ShapeDtypeStruct((m, D_MODEL), F32)]
        + [jax.ShapeDtypeStruct((m, D_MODEL), BF16)] * with_next,
        scratch_shapes=[pltpu.VMEM((tm, D_MODEL), BF16)],
        compiler_params=pltpu.CompilerParams(
            dimension_semantics=("parallel", "arbitrary"),
            vmem_limit_bytes=_vmem_limit(blocks, resident)),
        name="ffn",
    )(h, g_pre, w_gu, w_gu, w_down, g_post, *([g_next] * with_next))


def _dot_nt(x, wt):
    return lax.dot_general(x, wt, (((1,), (1,)), ((), ())), preferred_element_type=F32)


def _in_proj_kernel(u_ref, wt_ref, z_ref, *, tn):
    j = pl.program_id(0)
    is_q = (j >= P_Q * D_MODEL // tn) & (j < (P_Q + 1) * D_MODEL // tn)
    factor = jnp.where(is_q, QK_SCALE_LOG2, 1.0)
    z_ref[...] = (_dot_nt(u_ref[...], wt_ref[...]) * factor).astype(BF16)


def _in_proj(u, w_in_t, *, tm, tn):
    m = u.shape[0]
    blocks = tm * D_MODEL * 2 + D_MODEL * tn * 2 + tm * tn * 2
    resident = tm * tn * 4
    return pl.pallas_call(
        functools.partial(_in_proj_kernel, tn=tn),
        grid=(Z_COLS // tn, m // tm),
        in_specs=[
            pl.BlockSpec((tm, D_MODEL), lambda j, i: (i, 0)),
            pl.BlockSpec((tn, D_MODEL), lambda j, i: (j, 0)),
        ],
        out_specs=pl.BlockSpec((tm, tn), lambda j, i: (i, j)),
        out_shape=jax.ShapeDtypeStruct((m, Z_COLS), BF16),
        compiler_params=pltpu.CompilerParams(
            dimension_semantics=("parallel", "parallel"),
            vmem_limit_bytes=_vmem_limit(blocks, resident)),
        name="in_proj",
    )(u, w_in_t)


def _gate_proj_kernel(u_ref, wt_ref, wflt_ref, zg_ref, fl_ref):
    u = u_ref[...]
    zg_ref[...] = _dot_nt(u, wt_ref[...]).astype(BF16)

    @pl.when(pl.program_id(1) == 0)
    def _():
        fl_ref[...] = _dot_nt(u, wflt_ref[...])


def _gate_proj(u, w_hi_t, w_fl_t, *, tm, tn):
    m = u.shape[0]
    n_cols = w_hi_t.shape[0]
    blocks = tm * D_MODEL * 2 + D_MODEL * tn * 2 + D_MODEL * LANES * 2 + tm * tn * 2 + tm * LANES * 4
    resident = tm * tn * 4
    return pl.pallas_call(
        _gate_proj_kernel,
        grid=(m // tm, n_cols // tn),
        in_specs=[
            pl.BlockSpec((tm, D_MODEL), lambda i, j: (i, 0)),
            pl.BlockSpec((tn, D_MODEL), lambda i, j: (j, 0)),
            pl.BlockSpec((LANES, D_MODEL), lambda i, j: (0, 0)),
        ],
        out_specs=[
            pl.BlockSpec((tm, tn), lambda i, j: (i, j)),
            pl.BlockSpec((tm, LANES), lambda i, j: (i, 0)),
        ],
        out_shape=[
            jax.ShapeDtypeStruct((m, n_cols), BF16),
            jax.ShapeDtypeStruct((m, LANES), F32),
        ],
        compiler_params=pltpu.CompilerParams(
            dimension_semantics=("parallel", "arbitrary"),
            vmem_limit_bytes=_vmem_limit(blocks, resident)),
        name="gate_proj",
    )(u, w_hi_t, w_fl_t)


def _rnn_kernel(xr_ref, gr_ref, ctx_ref, h0_ref, cw_ref, cb_ref, wax_ref, ba_ref, bx_ref, lam_ref,
                y_ref, hlast_ref, xbuf, a_scr, b_scr, h_scr, carry_scr, *, tt, cb, pitch):
    t = pl.program_id(1)
    nslab = cb // LRU_BS

    @pl.when(t == 0)
    def _():
        xbuf[:, 0:SUBLANES, :] = jnp.broadcast_to(ctx_ref[...][None], (SUBLANES, SUBLANES, cb))
        carry_scr[...] = jnp.broadcast_to(h0_ref[...], (SUBLANES, cb))

    x = xr_ref[...].astype(F32)
    xbuf[:, SUBLANES:SUBLANES + tt, :] = x
    xc = xbuf[:, pl.ds(SUBLANES - 3, tt), :] * cw_ref[0:1, :]
    xc = xc + xbuf[:, pl.ds(SUBLANES - 2, tt), :] * cw_ref[1:2, :]
    xc = xc + xbuf[:, pl.ds(SUBLANES - 1, tt), :] * cw_ref[2:3, :]
    xc = xc + x * cw_ref[3:4, :]
    xc = xc + cb_ref[...]
    xbuf[:, 0:SUBLANES, :] = xbuf[:, tt:tt + SUBLANES, :]

    for n in range(nslab):
        cols = slice(n * LRU_BS, (n + 1) * LRU_BS)
        xn = xc[:, :, cols].reshape(SUBLANES * tt, LRU_BS)
        g = jnp.dot(xn.astype(BF16), wax_ref[n], preferred_element_type=F32)
        r = jax.nn.sigmoid(g[:, :LRU_BS] + ba_ref[:, cols])
        i = jax.nn.sigmoid(g[:, LRU_BS:] + bx_ref[:, cols])
        neg_lam = -lam_ref[:, cols]
        softplus = jnp.maximum(neg_lam, 0.0) + jnp.log1p(jnp.exp(-jnp.abs(neg_lam)))
        a = jnp.exp2((-LRU_C * LOG2E * softplus) * r)
        bv = jnp.sqrt(1.0 - a * a) * (i * xn)
        for b in range(SUBLANES):
            a_scr[n, b * pitch:b * pitch + tt, :] = a[b * tt:(b + 1) * tt]
            b_scr[n, b * pitch:b * pitch + tt, :] = bv[b * tt:(b + 1) * tt]

    def step(tl, hs):
        rows = pl.ds(tl, SUBLANES, stride=pitch)
        hs = tuple(a_scr[n, rows, :] * hs[n] + b_scr[n, rows, :] for n in range(nslab))
        for n in range(nslab):
            h_scr[n, rows, :] = hs[n]
        return hs

    hs = tuple(carry_scr[:, n * LRU_BS:(n + 1) * LRU_BS] for n in range(nslab))
    hs = lax.fori_loop(0, tt, step, hs, unroll=8)
    for n in range(nslab):
        cols = slice(n * LRU_BS, (n + 1) * LRU_BS)
        carry_scr[:, cols] = hs[n]
        hlast_ref[:, cols] = hs[n]
        for b in range(SUBLANES):
            gate = jax.nn.gelu(gr_ref[b, :, cols].astype(F32), approximate=True)
            y_ref[b, :, cols] = (h_scr[n, b * pitch:b * pitch + tt, :] * gate).astype(BF16)


def _rnn(z3, ctx, h0, conv_w, conv_b, w_ax, b_a, b_x, lam, *, tt, cb):
    b, t_len, _ = z3.shape
    assert b == SUBLANES
    ncb = D_MODEL // cb
    nblk = cb // LRU_BS
    pitch = tt + SUBLANES if (tt // SUBLANES) % 2 == 0 else tt + 2 * SUBLANES
    vec = lambda rows: pl.BlockSpec((rows, cb), lambda ci, ti: (0, ci))
    blocks = 3 * b * tt * cb * 2 + nblk * LRU_BS * 2 * LRU_BS * 2 + b * cb * 4
    resident = b * (tt + SUBLANES) * cb * 4 + 3 * b * pitch * cb * 4 + 6 * b * tt * cb * 4
    return pl.pallas_call(
        functools.partial(_rnn_kernel, tt=tt, cb=cb, pitch=pitch),
        grid=(ncb, t_len // tt),
        in_specs=[
            pl.BlockSpec((b, tt, cb), lambda ci, ti: (0, ti, P_XR * ncb + ci)),
            pl.BlockSpec((b, tt, cb), lambda ci, ti: (0, ti, P_GR * ncb + ci)),
            vec(SUBLANES), vec(1), vec(CONV_W), vec(1),
            pl.BlockSpec((nblk, LRU_BS, 2 * LRU_BS), lambda ci, ti: (ci, 0, 0)),
            vec(1), vec(1), vec(1),
        ],
        out_specs=[
            pl.BlockSpec((b, tt, cb), lambda ci, ti: (0, ti, ci)),
            pl.BlockSpec((b, cb), lambda ci, ti: (0, ci)),
        ],
        out_shape=[
            jax.ShapeDtypeStruct((b, t_len, D_MODEL), BF16),
            jax.ShapeDtypeStruct((b, D_MODEL), F32),
        ],
        scratch_shapes=[
            pltpu.VMEM((b, tt + SUBLANES, cb), F32),
            pltpu.VMEM((nblk, b * pitch, LRU_BS), F32),
            pltpu.VMEM((nblk, b * pitch, LRU_BS), F32),
            pltpu.VMEM((nblk, b * pitch, LRU_BS), F32),
            pltpu.VMEM((b, cb), F32),
        ],
        compiler_params=pltpu.CompilerParams(
            dimension_semantics=("parallel", "arbitrary"),
            vmem_limit_bytes=_vmem_limit(blocks, resident)),
        name="rnn",
    )(z3, z3, ctx, h0, conv_w, conv_b, w_ax, b_a, b_x, lam)


def _forget_kernel(x_ref, fb_ref, nf_ref, *, n_blocks, first_valid):
    lane = lax.broadcasted_iota(jnp.int32, (N_HEADS, LANES), 1)
    fb = fb_ref[...]
    run = jnp.zeros((N_HEADS, 1), F32)
    for k in range(n_blocks):
        x = x_ref[0, :, k * LANES:(k + 1) * LANES] + fb
        lf = jnp.minimum(x, 0.0) - jnp.log1p(jnp.exp(-jnp.abs(x)))
        if k == 0:
            lf = jnp.where(lane >= first_valid, lf, 0.0)
        d = 1
        while d < LANES:
            lf = jnp.where(lane >= d, lf + pltpu.roll(lf, d, axis=1), lf)
            d *= 2
        f = lf + run
        run = f[:, LANES - 1:LANES]
        neg = f * -LOG2E
        if k == 0:
            neg = jnp.where(lane >= first_valid, neg, NEG_BIG)
        nf_ref[0, :, k * LANES:(k + 1) * LANES] = neg


def _forget_prefix(x_all, forget_b):
    b, _, n = x_all.shape
    return pl.pallas_call(
        functools.partial(_forget_kernel, n_blocks=n // LANES, first_valid=LANES - N_META),
        grid=(b,),
        in_specs=[
            pl.BlockSpec((1, N_HEADS, n), lambda bi: (bi, 0, 0)),
            pl.BlockSpec((N_HEADS, 1), lambda bi: (0, 0)),
        ],
        out_specs=pl.BlockSpec((1, N_HEADS, n), lambda bi: (bi, 0, 0)),
        out_shape=jax.ShapeDtypeStruct((b, N_HEADS, n), F32),
        compiler_params=pltpu.CompilerParams(dimension_semantics=("parallel",)),
        name="forget_prefix",
    )(x_all, forget_b)


def _attn_kernel(q_ref, k_ref, v_ref, km_ref, vm_ref, nf_ref, o_ref, vx_scr, vmx_scr, *, tq, nq, hg):
    qi = pl.program_id(2)
    causal = (lax.broadcasted_iota(jnp.int32, (tq, tq), 0)
              >= lax.broadcasted_iota(jnp.int32, (tq, tq), 1))

    @pl.when(qi == 0)
    def _():
        for hh in range(hg):
            cols = slice(hh * HEAD_DIM, (hh + 1) * HEAD_DIM)
            vx_scr[hh, :, 0:HEAD_DIM] = v_ref[0, :, cols]
            vx_scr[hh, :, HEAD_DIM:] = jnp.ones((vx_scr.shape[1], HEAD_DIM), BF16)
            vmx_scr[hh, :, 0:HEAD_DIM] = vm_ref[:, cols]
            vmx_scr[hh, :, HEAD_DIM:] = jnp.ones((LANES, HEAD_DIM), BF16)

    def head(hh, c):
        cols = slice(hh * HEAD_DIM, (hh + 1) * HEAD_DIM)
        q = q_ref[0, :, cols]

        def scores(k_blk, neg_f):
            s = lax.dot_general(q, k_blk, (((1,), (1,)), ((), ())), preferred_element_type=F32)
            return s + neg_f

        lo = c * tq
        ss = [scores(km_ref[:, cols], nf_ref[hh, :, 0:LANES])]
        vs = [vmx_scr[hh]]
        if c > 0:
            ss.append(scores(k_ref[0, 0:lo, cols], nf_ref[hh, :, LANES:LANES + lo]))
            vs.append(vx_scr[hh, 0:lo, :])
        s_diag = scores(k_ref[0, lo:lo + tq, cols], nf_ref[hh, :, LANES + lo:LANES + lo + tq])
        ss.append(jnp.where(causal, s_diag, NEG_BIG))
        vs.append(vx_scr[hh, lo:lo + tq, :])
        m = functools.reduce(jnp.maximum, [jnp.max(s, axis=1, keepdims=True) for s in ss])
        ol = jnp.zeros((tq, 2 * HEAD_DIM), F32)
        for s, v_blk in zip(ss, vs):
            ol = ol + jnp.dot(jnp.exp2(s - m).astype(BF16), v_blk, preferred_element_type=F32)
        o_ref[0, :, cols] = (ol[:, :HEAD_DIM] / ol[:, HEAD_DIM:]).astype(BF16)

    for c in range(nq):
        @pl.when(qi == c)
        def _(c=c):
            for hh in range(hg):
                head(hh, c)


def _attention(z3, zm_pad, neg_f, *, tq, hg):
    b, t_len, _ = z3.shape
    nq = t_len // tq
    ng = N_HEADS // hg
    w = hg * HEAD_DIM
    blocks = (tq * w * 2 * 2 + 2 * t_len * w * 2 + 2 * LANES * w * 2
              + hg * SUBLANES * (LANES + t_len) * 4)
    resident = hg * 4 * tq * (t_len + LANES) * 4 + hg * (t_len + LANES) * 2 * HEAD_DIM * 2
    return pl.pallas_call(
        functools.partial(_attn_kernel, tq=tq, nq=nq, hg=hg),
        grid=(b, ng, nq),
        in_specs=[
            pl.BlockSpec((1, tq, w), lambda bi, gi, qi: (bi, qi, P_Q * ng + gi)),
            pl.BlockSpec((1, t_len, w), lambda bi, gi, qi: (bi, 0, P_K * ng + gi)),
            pl.BlockSpec((1, t_len, w), lambda bi, gi, qi: (bi, 0, P_V * ng + gi)),
            pl.BlockSpec((LANES, w), lambda bi, gi, qi: (0, P_K * ng + gi)),
            pl.BlockSpec((LANES, w), lambda bi, gi, qi: (0, P_V * ng + gi)),
            pl.BlockSpec((hg, 1, LANES + t_len), lambda bi, gi, qi: (bi * ng + gi, 0, 0)),
        ],
        out_specs=pl.BlockSpec((1, tq, w), lambda bi, gi, qi: (bi, qi, gi)),
        out_shape=jax.ShapeDtypeStruct((b, t_len, N_HEADS * HEAD_DIM), BF16),
        scratch_shapes=[
            pltpu.VMEM((hg, t_len, 2 * HEAD_DIM), BF16),
            pltpu.VMEM((hg, LANES, 2 * HEAD_DIM), BF16),
        ],
        compiler_params=pltpu.CompilerParams(
            dimension_semantics=("parallel", "parallel", "arbitrary"),
            vmem_limit_bytes=_vmem_limit(blocks, resident)),
        name="attention",
    )(z3, z3, z3, zm_pad, zm_pad, neg_f)


def _merge_kernel(yr_ref, ya_ref, wr_ref, wa_ref, gr_ref, ga_ref, o_ref):
    pr = jnp.dot(yr_ref[...], wr_ref[...], preferred_element_type=F32)
    pa = jnp.dot(ya_ref[...], wa_ref[...], preferred_element_type=F32)
    merged = (jax.nn.sigmoid(gr_ref[...].astype(F32)) * pr
              + jax.nn.sigmoid(ga_ref[...].astype(F32)) * pa)
    o_ref[...] = merged.astype(BF16)


def _merge(y_rnn, y_attn, w_rnn_out, w_attn_out, zg, *, tm, tn):
    m = y_rnn.shape[0]
    nn = D_MODEL // tn
    blocks = 2 * tm * D_MODEL * 2 + 2 * D_MODEL * tn * 2 + 3 * tm * tn * 2
    resident = 3 * tm * tn * 4
    return pl.pallas_call(
        _merge_kernel,
        grid=(m // tm, nn),
        in_specs=[
            pl.BlockSpec((tm, D_MODEL), lambda i, j: (i, 0)),
            pl.BlockSpec((tm, D_MODEL), lambda i, j: (i, 0)),
            pl.BlockSpec((D_MODEL, tn), lambda i, j: (0, j)),
            pl.BlockSpec((D_MODEL, tn), lambda i, j: (0, j)),
            pl.BlockSpec((tm, tn), lambda i, j: (i, P_GRNN * nn + j)),
            pl.BlockSpec((tm, tn), lambda i, j: (i, P_GATTN * nn + j)),
        ],
        out_specs=pl.BlockSpec((tm, tn), lambda i, j: (i, j)),
        out_shape=jax.ShapeDtypeStruct((m, D_MODEL), BF16),
        compiler_params=pltpu.CompilerParams(
            dimension_semantics=("parallel", "arbitrary"),
            vmem_limit_bytes=_vmem_limit(blocks, resident)),
        name="merge",
    )(y_rnn, y_attn, w_rnn_out, w_attn_out, zg, zg)


def _out_proj_kernel(x_ref, w_ref, h_ref, g_ref, o_ref):
    mix = jnp.dot(x_ref[...], w_ref[...], preferred_element_type=F32)
    o_ref[...] = h_ref[...] + _rms(mix, g_ref[...])


def _out_proj(merged, w_o, h, g_post, *, tm):
    m = merged.shape[0]
    blocks = tm * D_MODEL * (2 + 4 + 4) + D_MODEL * D_MODEL * 2
    resident = tm * D_MODEL * 4
    return pl.pallas_call(
        _out_proj_kernel,
        grid=(m // tm,),
        in_specs=[
            pl.BlockSpec((tm, D_MODEL), lambda i: (i, 0)),
            pl.BlockSpec((D_MODEL, D_MODEL), lambda i: (0, 0)),
            pl.BlockSpec((tm, D_MODEL), lambda i: (i, 0)),
            pl.BlockSpec((1, D_MODEL), lambda i: (0, 0)),
        ],
        out_specs=pl.BlockSpec((tm, D_MODEL), lambda i: (i, 0)),
        out_shape=jax.ShapeDtypeStruct((m, D_MODEL), F32),
        compiler_params=pltpu.CompilerParams(
            dimension_semantics=("parallel",),
            vmem_limit_bytes=_vmem_limit(blocks, resident)),
        name="out_proj",
    )(merged, w_o, h, g_post)


def kernel(x, meta_tokens, norm_g, ffn1_w_gu, ffn1_w_down, w_in, conv_w, conv_b, lru_w_a, lru_b_a,
           lru_w_x, lru_b_x, lru_lambda, forget_b, w_rnn_out, w_attn_out, w_o, ffn2_w_gu,
           ffn2_w_down):
    b, t_len, d = x.shape
    m = b * t_len
    g = [norm_g[0, k:k + 1] for k in range(6)]
    row = lambda v: v[0].reshape(1, -1)

    w_in_t = w_in[0].T.astype(BF16)
    w_fl_t = w_in_t[Z_COLS:Z_COLS + LANES]
    w_hi_t = w_in_t[Z_COLS + N_HEADS:]
    w_ax = jnp.concatenate([lru_w_a[0], lru_w_x[0]], axis=-1).astype(BF16)
    w_gu1, w_d1 = ffn1_w_gu[0].astype(BF16), ffn1_w_down[0].astype(BF16)
    w_gu2, w_d2 = ffn2_w_gu[0].astype(BF16), ffn2_w_down[0].astype(BF16)
    w_r, w_a, w_out = w_rnn_out[0].astype(BF16), w_attn_out[0].astype(BF16), w_o[0].astype(BF16)
    rnn_params = (conv_w[0], row(conv_b), w_ax, row(lru_b_a), row(lru_b_x), row(lru_lambda))

    _, u_m = _ffn(meta_tokens, g[0], w_gu1, w_d1, g[1], g[2], tm=N_META, h_buffers=2)
    z_m = _in_proj(u_m, w_in_t, tm=N_META, tn=2048)
    _, fl_m = _gate_proj(u_m, w_hi_t[:LANES], w_fl_t, tm=N_META, tn=LANES)
    zeros = jnp.zeros((SUBLANES, D_MODEL), F32)
    _, h_meta = _rnn(jnp.broadcast_to(z_m[None], (SUBLANES, N_META, Z_COLS)), zeros, zeros[:1],
                     *rnn_params, tt=N_META, cb=512)
    ctx = z_m[N_META - SUBLANES:, :D_MODEL].astype(F32)
    zm_pad = jnp.pad(z_m, ((LANES - N_META, 0), (0, 0)))

    h1, u2 = _ffn(x.reshape(m, d), g[0], w_gu1, w_d1, g[1], g[2], tm=512, h_buffers=2)
    z = _in_proj(u2, w_in_t, tm=1024, tn=2048)
    zg, fl = _gate_proj(u2, w_hi_t, w_fl_t, tm=1024, tn=2048)
    z3 = z.reshape(b, t_len, Z_COLS)
    y_rnn, _ = _rnn(z3, ctx, h_meta[:1], *rnn_params, tt=256, cb=512)

    fl_meta = jnp.pad(fl_m[:, :N_HEADS].T, ((0, 0), (LANES - N_META, 0)))
    fl_real = fl[:, :N_HEADS].reshape(b, t_len, N_HEADS).transpose(0, 2, 1)
    x_all = jnp.concatenate([jnp.broadcast_to(fl_meta[None], (b, N_HEADS, LANES)), fl_real], axis=2)
    neg_f = _forget_prefix(x_all, forget_b[0].reshape(N_HEADS, 1))
    y_attn = _attention(z3, zm_pad, neg_f.reshape(b * N_HEADS, 1, LANES + t_len), tq=512, hg=4)

    merged = _merge(y_rnn.reshape(m, d), y_attn.reshape(m, d), w_r, w_a, zg, tm=1024, tn=512)
    h2 = _out_proj(merged, w_out, h1, g[3], tm=512)
    (h3,) = _ffn(h2, g[4], w_gu2, w_d2, g[5], None, tm=512, h_buffers=2)
    return h3.reshape(b, t_len, d)
```

```python
import functools

import jax
import jax.numpy as jnp
from jax import lax
from jax.experimental import pallas as pl
from jax.experimental.pallas import tpu as pltpu

D_MODEL = 2048
N_META = 16
D_FF = 5632
N_HEADS = 16
HEAD_DIM = 128
LRU_BS = 128
LRU_BLOCKS = D_MODEL // LRU_BS
CONV_W = 4
LRU_C = 8.0
EPS = 1e-6
(P_XR, P_GR, P_Q, P_K, P_V) = range(5)
Z_COLS = 5 * D_MODEL
(P_GRNN, P_GATTN) = range(2)
ZG_COLS = 2 * D_MODEL
FFN_TM, FFN_TF = 512, 512
PROJ_TM, PROJ_TN = 1024, 2048
RNN_TT, RNN_CB = 256, 512
ATTN_TQ, ATTN_HEADS_PER_STEP = 512, 4
MERGE_TM, MERGE_TN = 1024, 512
OUT_TM = 512

LANES = 128
SUBLANES = 8
VMEM_BYTES_V7X = 64 * 1024 * 1024
NEG_BIG = -1e30
LOG2E = 1.4426950408889634
QK_SCALE_LOG2 = HEAD_DIM ** -0.5 * LOG2E

F32 = jnp.float32
BF16 = jnp.bfloat16


def _vmem_limit(pipelined_bytes, resident_bytes):
    return min(2 * pipelined_bytes + resident_bytes + (4 << 20), VMEM_BYTES_V7X - (4 << 20))


def _rms(x, g):
    return (x * lax.rsqrt(jnp.mean(x * x, axis=-1, keepdims=True) + EPS)) * g


def _ffn_kernel(h_ref, gpre_ref, wg_ref, wu_ref, wd_ref, gpost_ref, *rest, with_next):
    if with_next:
        gnext_ref, out_ref, unext_ref, u_scr = rest
    else:
        out_ref, u_scr = rest
    j = pl.program_id(1)

    @pl.when(j == 0)
    def _():
        u_scr[...] = _rms(h_ref[...], gpre_ref[...]).astype(BF16)
        out_ref[...] = jnp.zeros_like(out_ref)

    u = u_scr[...]
    gate = jnp.dot(u, wg_ref[...], preferred_element_type=F32)
    up = jnp.dot(u, wu_ref[...], preferred_element_type=F32)
    act = ((gate * jax.nn.sigmoid(gate)) * up).astype(BF16)
    out_ref[...] += jnp.dot(act, wd_ref[...], preferred_element_type=F32)

    @pl.when(j == pl.num_programs(1) - 1)
    def _():
        h_new = h_ref[...] + 0.5 * _rms(out_ref[...], gpost_ref[...])
        out_ref[...] = h_new
        if with_next:
            unext_ref[...] = _rms(h_new, gnext_ref[...]).astype(BF16)


def _ffn(h, g_pre, w_gu, w_down, g_post, g_next, *, tm):
    m = h.shape[0]
    tf = FFN_TF
    nf = D_FF // tf
    with_next = g_next is not None
    row_tile = pl.BlockSpec((tm, D_MODEL), lambda i, j: (i, 0))
    gain = pl.BlockSpec((1, D_MODEL), lambda i, j: (0, 0))
    blocks = tm * D_MODEL * (4 + 4 + 2 * with_next) + 3 * D_MODEL * tf * 2
    resident = tm * D_MODEL * (2 + 4) + tm * tf * (4 + 4 + 2)
    return pl.pallas_call(
        functools.partial(_ffn_kernel, with_next=with_next),
        grid=(m // tm, nf),
        in_specs=[
            row_tile,
            gain,
            pl.BlockSpec((D_MODEL, tf), lambda i, j: (0, j)),
            pl.BlockSpec((D_MODEL, tf), lambda i, j: (0, j + nf)),
            pl.BlockSpec((tf, D_MODEL), lambda i, j: (j, 0)),
            gain,
        ] + [gain] * with_next,
        out_specs=[row_tile] * (1 + with_next),
        out_shape=[jax.ShapeDtypeStruct((m, D_MODEL), F32)]
        + [jax.ShapeDtypeStruct((m, D_MODEL), BF16)] * with_next,
        scratch_shapes=[pltpu.VMEM((tm, D_MODEL), BF16)],
        compiler_params=pltpu.CompilerParams(
            dimension_semantics=("parallel", "arbitrary"),
            vmem_limit_bytes=_vmem_limit(blocks, resident)),
        name="ffn",
    )(h, g_pre, w_gu, w_gu, w_down, g_post, *([g_next] * with_next))


def _dot_nt(x, wt):
    return lax.dot_general(x, wt, (((1,), (1,)), ((), ())), preferred_element_type=F32)


def _in_proj_kernel(u_ref, wt_ref, z_ref, *, tn):
    j = pl.program_id(0)
    is_q = (j >= P_Q * D_MODEL // tn) & (j < (P_Q + 1) * D_MODEL // tn)
    factor = jnp.where(is_q, QK_SCALE_LOG2, 1.0)
    z_ref[...] = (_dot_nt(u_ref[...], wt_ref[...]) * factor).astype(BF16)


def _in_proj(u, w_in_t, *, tm, tn):
    m = u.shape[0]
    blocks = tm * D_MODEL * 2 + D_MODEL * tn * 2 + tm * tn * 2
    resident = tm * tn * 4
    return pl.pallas_call(
        functools.partial(_in_proj_kernel, tn=tn),
        grid=(Z_COLS // tn, m // tm),
        in_specs=[
            pl.BlockSpec((tm, D_MODEL), lambda j, i: (i, 0)),
            pl.BlockSpec((tn, D_MODEL), lambda j, i: (j, 0)),
        ],
        out_specs=pl.BlockSpec((tm, tn), lambda j, i: (i, j)),
        out_shape=jax.ShapeDtypeStruct((m, Z_COLS), BF16),
        compiler_params=pltpu.CompilerParams(
            dimension_semantics=("parallel", "parallel"),
            vmem_limit_bytes=_vmem_limit(blocks, resident)),
        name="in_proj",
    )(u, w_in_t)


def _gate_proj_kernel(u_ref, wt_ref, wflt_ref, zg_ref, fl_ref):
    u = u_ref[...]
    zg_ref[...] = _dot_nt(u, wt_ref[...]).astype(BF16)

    @pl.when(pl.program_id(1) == 0)
    def _():
        fl_ref[...] = _dot_nt(u, wflt_ref[...])


def _gate_proj(u, w_hi_t, w_fl_t, *, tm, tn):
    m = u.shape[0]
    n_cols = w_hi_t.shape[0]
    blocks = tm * D_MODEL * 2 + D_MODEL * tn * 2 + D_MODEL * LANES * 2 + tm * tn * 2 + tm * LANES * 4
    resident = tm * tn * 4
    return pl.pallas_call(
        _gate_proj_kernel,
        grid=(m // tm, n_cols // tn),
        in_specs=[
            pl.BlockSpec((tm, D_MODEL), lambda i, j: (i, 0)),
            pl.BlockSpec((tn, D_MODEL), lambda i, j: (j, 0)),
            pl.BlockSpec((LANES, D_MODEL), lambda i, j: (0, 0)),
        ],
        out_specs=[
            pl.BlockSpec((tm, tn), lambda i, j: (i, j)),
            pl.BlockSpec((tm, LANES), lambda i, j: (i, 0)),
        ],
        out_shape=[
            jax.ShapeDtypeStruct((m, n_cols), BF16),
            jax.ShapeDtypeStruct((m, LANES), F32),
        ],
        compiler_params=pltpu.CompilerParams(
            dimension_semantics=("parallel", "arbitrary"),
            vmem_limit_bytes=_vmem_limit(blocks, resident)),
        name="gate_proj",
    )(u, w_hi_t, w_fl_t)


def _gelu_tanh(x):
    c = (2.0 / jnp.pi) ** 0.5
    half = 0.5 * x
    return half + half * jnp.tanh(x * (c + (c * 0.044715) * (x * x)))


def _rnn_kernel(xr_ref, gr_ref, ctx_ref, h0_ref, cw_ref, cb_ref, wax_ref, ba_ref, bx_ref, lam_ref,
                y_ref, hlast_ref, xbuf, a_scr, b_scr, h_scr, carry_scr, *, tt, cb, pitch):
    t = pl.program_id(1)
    nslab = cb // LRU_BS

    @pl.when(t == 0)
    def _():
        xbuf[:, 0:SUBLANES, :] = jnp.broadcast_to(ctx_ref[...][None], (SUBLANES, SUBLANES, cb))
        carry_scr[...] = jnp.broadcast_to(h0_ref[...], (SUBLANES, cb))

    x = xr_ref[...].astype(F32)
    xbuf[:, SUBLANES:SUBLANES + tt, :] = x
    xc = xbuf[:, pl.ds(SUBLANES - 3, tt), :] * cw_ref[0:1, :]
    xc = xc + xbuf[:, pl.ds(SUBLANES - 2, tt), :] * cw_ref[1:2, :]
    xc = xc + xbuf[:, pl.ds(SUBLANES - 1, tt), :] * cw_ref[2:3, :]
    xc = xc + x * cw_ref[3:4, :]
    xc = xc + cb_ref[...]
    xbuf[:, 0:SUBLANES, :] = xbuf[:, tt:tt + SUBLANES, :]

    for n in range(nslab):
        cols = slice(n * LRU_BS, (n + 1) * LRU_BS)
        xn = xc[:, :, cols].reshape(SUBLANES * tt, LRU_BS)
        g = jnp.dot(xn.astype(BF16), wax_ref[n], preferred_element_type=F32)
        r = jax.nn.sigmoid(g[:, :LRU_BS] + ba_ref[:, cols])
        i = jax.nn.sigmoid(g[:, LRU_BS:] + bx_ref[:, cols])
        neg_lam = -lam_ref[:, cols]
        softplus = jnp.maximum(neg_lam, 0.0) + jnp.log1p(jnp.exp(-jnp.abs(neg_lam)))
        a = jnp.exp2((-LRU_C * LOG2E * softplus) * r)
        v = 1.0 - a * a
        bv = jnp.where(v > 0.0, v * lax.rsqrt(v), 0.0) * (i * xn)
        for b in range(SUBLANES):
            a_scr[n, b * pitch:b * pitch + tt, :] = a[b * tt:(b + 1) * tt]
            b_scr[n, b * pitch:b * pitch + tt, :] = bv[b * tt:(b + 1) * tt]

    def step(tl, hs):
        rows = pl.ds(tl, SUBLANES, stride=pitch)
        hs = tuple(a_scr[n, rows, :] * hs[n] + b_scr[n, rows, :] for n in range(nslab))
        for n in range(nslab):
            h_scr[n, rows, :] = hs[n]
        return hs

    hs = tuple(carry_scr[:, n * LRU_BS:(n + 1) * LRU_BS] for n in range(nslab))
    hs = lax.fori_loop(0, tt, step, hs, unroll=8)
    for n in range(nslab):
        cols = slice(n * LRU_BS, (n + 1) * LRU_BS)
        carry_scr[:, cols] = hs[n]
        hlast_ref[:, cols] = hs[n]
        for b in range(SUBLANES):
            gate = _gelu_tanh(gr_ref[b, :, cols].astype(F32))
            y_ref[b, :, cols] = (h_scr[n, b * pitch:b * pitch + tt, :] * gate).astype(BF16)


def _rnn(z3, ctx, h0, conv_w, conv_b, w_ax, b_a, b_x, lam, *, tt, cb):
    b, t_len, _ = z3.shape
    assert b == SUBLANES
    ncb = D_MODEL // cb
    nblk = cb // LRU_BS
    pitch = tt + SUBLANES if (tt // SUBLANES) % 2 == 0 else tt + 2 * SUBLANES
    vec = lambda rows: pl.BlockSpec((rows, cb), lambda ci, ti: (0, ci))
    blocks = 3 * b * tt * cb * 2 + nblk * LRU_BS * 2 * LRU_BS * 2 + b * cb * 4
    resident = b * (tt + SUBLANES) * cb * 4 + 3 * b * pitch * cb * 4 + 6 * b * tt * cb * 4
    return pl.pallas_call(
        functools.partial(_rnn_kernel, tt=tt, cb=cb, pitch=pitch),
        grid=(ncb, t_len // tt),
        in_specs=[
            pl.BlockSpec((b, tt, cb), lambda ci, ti: (0, ti, P_XR * ncb + ci)),
            pl.BlockSpec((b, tt, cb), lambda ci, ti: (0, ti, P_GR * ncb + ci)),
            vec(SUBLANES), vec(1), vec(CONV_W), vec(1),
            pl.BlockSpec((nblk, LRU_BS, 2 * LRU_BS), lambda ci, ti: (ci, 0, 0)),
            vec(1), vec(1), vec(1),
        ],
        out_specs=[
            pl.BlockSpec((b, tt, cb), lambda ci, ti: (0, ti, ci)),
            pl.BlockSpec((b, cb), lambda ci, ti: (0, ci)),
        ],
        out_shape=[
            jax.ShapeDtypeStruct((b, t_len, D_MODEL), BF16),
            jax.ShapeDtypeStruct((b, D_MODEL), F32),
        ],
        scratch_shapes=[
            pltpu.VMEM((b, tt + SUBLANES, cb), F32),
            pltpu.VMEM((nblk, b * pitch, LRU_BS), F32),
            pltpu.VMEM((nblk, b * pitch, LRU_BS), F32),
            pltpu.VMEM((nblk, b * pitch, LRU_BS), F32),
            pltpu.VMEM((b, cb), F32),
        ],
        compiler_params=pltpu.CompilerParams(
            dimension_semantics=("parallel", "arbitrary"),
            vmem_limit_bytes=_vmem_limit(blocks, resident)),
        name="rnn",
    )(z3, z3, ctx, h0, conv_w, conv_b, w_ax, b_a, b_x, lam)


def _forget_kernel(x_ref, fb_ref, nf_ref, *, n_blocks, first_valid):
    lane = lax.broadcasted_iota(jnp.int32, (N_HEADS, LANES), 1)
    fb = fb_ref[...]
    run = jnp.zeros((N_HEADS, 1), F32)
    for k in range(n_blocks):
        x = x_ref[0, :, k * LANES:(k + 1) * LANES] + fb
        lf = jnp.minimum(x, 0.0) - jnp.log1p(jnp.exp(-jnp.abs(x)))
        if k == 0:
            lf = jnp.where(lane >= first_valid, lf, 0.0)
        d = 1
        while d < LANES:
            lf = jnp.where(lane >= d, lf + pltpu.roll(lf, d, axis=1), lf)
            d *= 2
        f = lf + run
        run = f[:, LANES - 1:LANES]
        neg = f * -LOG2E
        if k == 0:
            neg = jnp.where(lane >= first_valid, neg, NEG_BIG)
        nf_ref[0, :, k * LANES:(k + 1) * LANES] = neg


def _forget_prefix(x_all, forget_b):
    b, _, n = x_all.shape
    return pl.pallas_call(
        functools.partial(_forget_kernel, n_blocks=n // LANES, first_valid=LANES - N_META),
        grid=(b,),
        in_specs=[
            pl.BlockSpec((1, N_HEADS, n), lambda bi: (bi, 0, 0)),
            pl.BlockSpec((N_HEADS, 1), lambda bi: (0, 0)),
        ],
        out_specs=pl.BlockSpec((1, N_HEADS, n), lambda bi: (bi, 0, 0)),
        out_shape=jax.ShapeDtypeStruct((b, N_HEADS, n), F32),
        compiler_params=pltpu.CompilerParams(dimension_semantics=("parallel",)),
        name="forget_prefix",
    )(x_all, forget_b)


def _attn_kernel(q_ref, k_ref, v_ref, km_ref, vm_ref, nf_ref, o_ref, vx_scr, vmx_scr, *, tq, nq, hg):
    qi = pl.program_id(2)
    causal = (lax.broadcasted_iota(jnp.int32, (tq, tq), 0)
              >= lax.broadcasted_iota(jnp.int32, (tq, tq), 1))

    @pl.when(qi == 0)
    def _():
        for hh in range(hg):
            cols = slice(hh * HEAD_DIM, (hh + 1) * HEAD_DIM)
            vx_scr[hh, :, 0:HEAD_DIM] = v_ref[0, :, cols]
            vx_scr[hh, :, HEAD_DIM:] = jnp.ones((vx_scr.shape[1], HEAD_DIM), BF16)
            vmx_scr[hh, :, 0:HEAD_DIM] = vm_ref[:, cols]
            vmx_scr[hh, :, HEAD_DIM:] = jnp.ones((LANES, HEAD_DIM), BF16)

    def head(hh, c):
        cols = slice(hh * HEAD_DIM, (hh + 1) * HEAD_DIM)
        q = q_ref[0, :, cols]

        def scores(k_blk, neg_f):
            s = lax.dot_general(q, k_blk, (((1,), (1,)), ((), ())), preferred_element_type=F32)
            return s + neg_f

        lo = c * tq
        ss = [scores(km_ref[:, cols], nf_ref[hh, :, 0:LANES])]
        vs = [vmx_scr[hh]]
        if c > 0:
            ss.append(scores(k_ref[0, 0:lo, cols], nf_ref[hh, :, LANES:LANES + lo]))
            vs.append(vx_scr[hh, 0:lo, :])
        s_diag = scores(k_ref[0, lo:lo + tq, cols], nf_ref[hh, :, LANES + lo:LANES + lo + tq])
        ss.append(jnp.where(causal, s_diag, NEG_BIG))
        vs.append(vx_scr[hh, lo:lo + tq, :])
        m = functools.reduce(jnp.maximum, [jnp.max(s, axis=1, keepdims=True) for s in ss])
        ol = jnp.zeros((tq, 2 * HEAD_DIM), F32)
        for s, v_blk in zip(ss, vs):
            ol = ol + jnp.dot(jnp.exp2(s - m).astype(BF16), v_blk, preferred_element_type=F32)
        o_ref[0, :, cols] = (ol[:, :HEAD_DIM] / ol[:, HEAD_DIM:]).astype(BF16)

    for c in range(nq):
        @pl.when(qi == c)
        def _(c=c):
            for hh in range(hg):
                head(hh, c)


def _attention(z3, zm_pad, neg_f, *, tq, hg):
    b, t_len, _ = z3.shape
    nq = t_len // tq
    ng = N_HEADS // hg
    w = hg * HEAD_DIM
    blocks = (tq * w * 2 * 2 + 2 * t_len * w * 2 + 2 * LANES * w * 2
              + hg * SUBLANES * (LANES + t_len) * 4)
    resident = hg * 4 * tq * (t_len + LANES) * 4 + hg * (t_len + LANES) * 2 * HEAD_DIM * 2
    return pl.pallas_call(
        functools.partial(_attn_kernel, tq=tq, nq=nq, hg=hg),
        grid=(b, ng, nq),
        in_specs=[
            pl.BlockSpec((1, tq, w), lambda bi, gi, qi: (bi, qi, P_Q * ng + gi)),
            pl.BlockSpec((1, t_len, w), lambda bi, gi, qi: (bi, 0, P_K * ng + gi)),
            pl.BlockSpec((1, t_len, w), lambda bi, gi, qi: (bi, 0, P_V * ng + gi)),
            pl.BlockSpec((LANES, w), lambda bi, gi, qi: (0, P_K * ng + gi)),
            pl.BlockSpec((LANES, w), lambda bi, gi, qi: (0, P_V * ng + gi)),
            pl.BlockSpec((hg, 1, LANES + t_len), lambda bi, gi, qi: (bi * ng + gi, 0, 0)),
        ],
        out_specs=pl.BlockSpec((1, tq, w), lambda bi, gi, qi: (bi, qi, gi)),
        out_shape=jax.ShapeDtypeStruct((b, t_len, N_HEADS * HEAD_DIM), BF16),
        scratch_shapes=[
            pltpu.VMEM((hg, t_len, 2 * HEAD_DIM), BF16),
            pltpu.VMEM((hg, LANES, 2 * HEAD_DIM), BF16),
        ],
        compiler_params=pltpu.CompilerParams(
            dimension_semantics=("parallel", "parallel", "arbitrary"),
            vmem_limit_bytes=_vmem_limit(blocks, resident)),
        name="attention",
    )(z3, z3, z3, zm_pad, zm_pad, neg_f)


def _merge_kernel(yr_ref, ya_ref, wr_ref, wa_ref, gr_ref, ga_ref, o_ref):
    pr = jnp.dot(yr_ref[...], wr_ref[...], preferred_element_type=F32)
    pa = jnp.dot(ya_ref[...], wa_ref[...], preferred_element_type=F32)
    merged = (jax.nn.sigmoid(gr_ref[...].astype(F32)) * pr
              + jax.nn.sigmoid(ga_ref[...].astype(F32)) * pa)
    o_ref[...] = merged.astype(BF16)


def _merge(y_rnn, y_attn, w_rnn_out, w_attn_out, zg, *, tm, tn):
    m = y_rnn.shape[0]
    nn = D_MODEL // tn
    blocks = 2 * tm * D_MODEL * 2 + 2 * D_MODEL * tn * 2 + 3 * tm * tn * 2
    resident = 3 * tm * tn * 4
    return pl.pallas_call(
        _merge_kernel,
        grid=(m // tm, nn),
        in_specs=[
            pl.BlockSpec((tm, D_MODEL), lambda i, j: (i, 0)),
            pl.BlockSpec((tm, D_MODEL), lambda i, j: (i, 0)),
            pl.BlockSpec((D_MODEL, tn), lambda i, j: (0, j)),
            pl.BlockSpec((D_MODEL, tn), lambda i, j: (0, j)),
            pl.BlockSpec((tm, tn), lambda i, j: (i, P_GRNN * nn + j)),
            pl.BlockSpec((tm, tn), lambda i, j: (i, P_GATTN * nn + j)),
        ],
        out_specs=pl.BlockSpec((tm, tn), lambda i, j: (i, j)),
        out_shape=jax.ShapeDtypeStruct((m, D_MODEL), BF16),
        compiler_params=pltpu.CompilerParams(
            dimension_semantics=("parallel", "arbitrary"),
            vmem_limit_bytes=_vmem_limit(blocks, resident)),
        name="merge",
    )(y_rnn, y_attn, w_rnn_out, w_attn_out, zg, zg)


def _out_proj_kernel(x_ref, w_ref, h_ref, g_ref, o_ref):
    mix = jnp.dot(x_ref[...], w_ref[...], preferred_element_type=F32)
    o_ref[...] = h_ref[...] + _rms(mix, g_ref[...])


def _out_proj(merged, w_o, h, g_post, *, tm):
    m = merged.shape[0]
    blocks = tm * D_MODEL * (2 + 4 + 4) + D_MODEL * D_MODEL * 2
    resident = tm * D_MODEL * 4
    return pl.pallas_call(
        _out_proj_kernel,
        grid=(m // tm,),
        in_specs=[
            pl.BlockSpec((tm, D_MODEL), lambda i: (i, 0)),
            pl.BlockSpec((D_MODEL, D_MODEL), lambda i: (0, 0)),
            pl.BlockSpec((tm, D_MODEL), lambda i: (i, 0)),
            pl.BlockSpec((1, D_MODEL), lambda i: (0, 0)),
        ],
        out_specs=pl.BlockSpec((tm, D_MODEL), lambda i: (i, 0)),
        out_shape=jax.ShapeDtypeStruct((m, D_MODEL), F32),
        compiler_params=pltpu.CompilerParams(
            dimension_semantics=("parallel",),
            vmem_limit_bytes=_vmem_limit(blocks, resident)),
        name="out_proj",
    )(merged, w_o, h, g_post)


def kernel(x, meta_tokens, norm_g, ffn1_w_gu, ffn1_w_down, w_in, conv_w, conv_b, lru_w_a, lru_b_a,
           lru_w_x, lru_b_x, lru_lambda, forget_b, w_rnn_out, w_attn_out, w_o, ffn2_w_gu,
           ffn2_w_down):
    b, t_len, d = x.shape
    m = b * t_len
    g = [norm_g[0, k:k + 1] for k in range(6)]
    row = lambda v: v[0].reshape(1, -1)

    w_in_t = w_in[0].T.astype(BF16)
    w_fl_t = w_in_t[Z_COLS:Z_COLS + LANES]
    w_hi_t = w_in_t[Z_COLS + N_HEADS:]
    w_ax = jnp.concatenate([lru_w_a[0], lru_w_x[0]], axis=-1).astype(BF16)
    w_gu1, w_d1 = ffn1_w_gu[0].astype(BF16), ffn1_w_down[0].astype(BF16)
    w_gu2, w_d2 = ffn2_w_gu[0].astype(BF16), ffn2_w_down[0].astype(BF16)
    w_r, w_a, w_out = w_rnn_out[0].astype(BF16), w_attn_out[0].astype(BF16), w_o[0].astype(BF16)
    rnn_params = (conv_w[0], row(conv_b), w_ax, row(lru_b_a), row(lru_b_x), row(lru_lambda))

    _, u_m = _ffn(meta_tokens, g[0], w_gu1, w_d1, g[1], g[2], tm=N_META)
    z_m = _in_proj(u_m, w_in_t, tm=N_META, tn=PROJ_TN)
    _, fl_m = _gate_proj(u_m, w_hi_t[:LANES], w_fl_t, tm=N_META, tn=LANES)
    zeros = jnp.zeros((SUBLANES, D_MODEL), F32)
    _, h_meta = _rnn(jnp.broadcast_to(z_m[None], (SUBLANES, N_META, Z_COLS)), zeros, zeros[:1],
                     *rnn_params, tt=N_META, cb=RNN_CB)
    ctx = z_m[N_META - SUBLANES:, :D_MODEL].astype(F32)
    zm_pad = jnp.pad(z_m, ((LANES - N_META, 0), (0, 0)))

    h1, u2 = _ffn(x.reshape(m, d), g[0], w_gu1, w_d1, g[1], g[2], tm=FFN_TM)
    z = _in_proj(u2, w_in_t, tm=PROJ_TM, tn=PROJ_TN)
    zg, fl = _gate_proj(u2, w_hi_t, w_fl_t, tm=PROJ_TM, tn=PROJ_TN)
    z3 = z.reshape(b, t_len, Z_COLS)
    y_rnn, _ = _rnn(z3, ctx, h_meta[:1], *rnn_params, tt=RNN_TT, cb=RNN_CB)

    fl_meta = jnp.pad(fl_m[:, :N_HEADS].T, ((0, 0), (LANES - N_META, 0)))
    fl_real = fl[:, :N_HEADS].reshape(b, t_len, N_HEADS).transpose(0, 2, 1)
    x_all = jnp.concatenate([jnp.broadcast_to(fl_meta[None], (b, N_HEADS, LANES)), fl_real], axis=2)
    neg_f = _forget_prefix(x_all, forget_b[0].reshape(N_HEADS, 1))
    y_attn = _attention(z3, zm_pad, neg_f.reshape(b * N_HEADS, 1, LANES + t_len),
                        tq=ATTN_TQ, hg=ATTN_HEADS_PER_STEP)

    merged = _merge(y_rnn.reshape(m, d), y_attn.reshape(m, d), w_r, w_a, zg, tm=MERGE_TM, tn=MERGE_TN)
    h2 = _out_proj(merged, w_out, h1, g[3], tm=OUT_TM)
    (h3,) = _ffn(h2, g[4], w_gu2, w_d2, g[5], None, tm=FFN_TM)
    return h3.reshape(b, t_len, d)
```

```python
import functools

import jax
import jax.numpy as jnp
from jax import lax
from jax.experimental import pallas as pl
from jax.experimental.pallas import tpu as pltpu

D_MODEL = 2048
N_META = 16
D_FF = 5632
N_HEADS = 16
HEAD_DIM = 128
LRU_BS = 128
LRU_BLOCKS = D_MODEL // LRU_BS
CONV_W = 4
LRU_C = 8.0
EPS = 1e-6
(P_XR, P_GR, P_Q, P_K, P_V) = range(5)
Z_COLS = 5 * D_MODEL
(P_GRNN, P_GATTN) = range(2)
ZG_COLS = 2 * D_MODEL
FFN_TM, FFN_TF = 512, 512
FFN_WEIGHT_BUFFERS = 3
PROJ_TM, PROJ_TN = 1024, 2048
RNN_TT, RNN_CB = 256, 512
ATTN_TQ, ATTN_HEADS_PER_STEP = 512, 4
MERGE_TM, MERGE_TN = 1024, 512
OUT_TM = 512

LANES = 128
SUBLANES = 8
VMEM_BYTES_V7X = 64 * 1024 * 1024
NEG_BIG = -1e30
LOG2E = 1.4426950408889634
QK_SCALE_LOG2 = HEAD_DIM ** -0.5 * LOG2E

F32 = jnp.float32
BF16 = jnp.bfloat16


def _vmem_limit(pipelined_bytes, resident_bytes):
    return min(2 * pipelined_bytes + resident_bytes + (4 << 20), VMEM_BYTES_V7X - (4 << 20))


def _rms(x, g):
    return (x * lax.rsqrt(jnp.mean(x * x, axis=-1, keepdims=True) + EPS)) * g


def _ffn_kernel(h_ref, gpre_ref, wg_ref, wu_ref, wd_ref, gpost_ref, *rest, with_next):
    if with_next:
        gnext_ref, out_ref, unext_ref, u_scr = rest
    else:
        out_ref, u_scr = rest
    j = pl.program_id(1)

    @pl.when(j == 0)
    def _():
        u_scr[...] = _rms(h_ref[...], gpre_ref[...]).astype(BF16)
        out_ref[...] = jnp.zeros_like(out_ref)

    u = u_scr[...]
    gate = jnp.dot(u, wg_ref[...], preferred_element_type=F32)
    up = jnp.dot(u, wu_ref[...], preferred_element_type=F32)
    act = ((gate * jax.nn.sigmoid(gate)) * up).astype(BF16)
    out_ref[...] += jnp.dot(act, wd_ref[...], preferred_element_type=F32)

    @pl.when(j == pl.num_programs(1) - 1)
    def _():
        h_new = h_ref[...] + 0.5 * _rms(out_ref[...], gpost_ref[...])
        out_ref[...] = h_new
        if with_next:
            unext_ref[...] = _rms(h_new, gnext_ref[...]).astype(BF16)


def _ffn(h, g_pre, w_gu, w_down, g_post, g_next, *, tm):
    m = h.shape[0]
    tf = FFN_TF
    nf = D_FF // tf
    with_next = g_next is not None
    row_tile = pl.BlockSpec((tm, D_MODEL), lambda i, j: (i, 0))
    gain = pl.BlockSpec((1, D_MODEL), lambda i, j: (0, 0))
    stream = pl.Buffered(FFN_WEIGHT_BUFFERS)
    in_specs = [
        row_tile,
        gain,
        pl.BlockSpec((D_MODEL, tf), lambda i, j: (0, j), pipeline_mode=stream),
        pl.BlockSpec((D_MODEL, tf), lambda i, j: (0, j + nf), pipeline_mode=stream),
        pl.BlockSpec((tf, D_MODEL), lambda i, j: (j, 0), pipeline_mode=stream),
        gain,
    ] + [gain] * with_next
    out_specs = [row_tile] * (1 + with_next)
    blocks = tm * D_MODEL * (4 + 4 + 2 * with_next)
    resident = (3 * D_MODEL * tf * 2 * FFN_WEIGHT_BUFFERS + tm * D_MODEL * (2 + 4)
                + tm * tf * (4 + 4 + 2))

    def pipelined(*refs):
        *operands, u_scr = refs
        pltpu.emit_pipeline(
            functools.partial(_ffn_kernel, with_next=with_next),
            grid=(m // tm, nf), in_specs=in_specs, out_specs=out_specs,
        )(*operands, scratches=(u_scr,))

    hbm = pl.BlockSpec(memory_space=pl.ANY)
    return pl.pallas_call(
        pipelined,
        in_specs=[hbm] * len(in_specs),
        out_specs=[hbm] * len(out_specs),
        out_shape=[jax.ShapeDtypeStruct((m, D_MODEL), F32)]
        + [jax.ShapeDtypeStruct((m, D_MODEL), BF16)] * with_next,
        scratch_shapes=[pltpu.VMEM((tm, D_MODEL), BF16)],
        compiler_params=pltpu.CompilerParams(vmem_limit_bytes=_vmem_limit(blocks, resident)),
        name="ffn",
    )(h, g_pre, w_gu, w_gu, w_down, g_post, *([g_next] * with_next))


def _dot_nt(x, wt):
    return lax.dot_general(x, wt, (((1,), (1,)), ((), ())), preferred_element_type=F32)


def _in_proj_kernel(u_ref, wt_ref, z_ref, *, tn):
    j = pl.program_id(0)
    is_q = (j >= P_Q * D_MODEL // tn) & (j < (P_Q + 1) * D_MODEL // tn)
    factor = jnp.where(is_q, QK_SCALE_LOG2, 1.0)
    z_ref[...] = (_dot_nt(u_ref[...], wt_ref[...]) * factor).astype(BF16)


def _in_proj(u, w_in_t, *, tm, tn):
    m = u.shape[0]
    blocks = tm * D_MODEL * 2 + D_MODEL * tn * 2 + tm * tn * 2
    resident = tm * tn * 4
    return pl.pallas_call(
        functools.partial(_in_proj_kernel, tn=tn),
        grid=(Z_COLS // tn, m // tm),
        in_specs=[
            pl.BlockSpec((tm, D_MODEL), lambda j, i: (i, 0)),
            pl.BlockSpec((tn, D_MODEL), lambda j, i: (j, 0)),
        ],
        out_specs=pl.BlockSpec((tm, tn), lambda j, i: (i, j)),
        out_shape=jax.ShapeDtypeStruct((m, Z_COLS), BF16),
        compiler_params=pltpu.CompilerParams(
            dimension_semantics=("parallel", "parallel"),
            vmem_limit_bytes=_vmem_limit(blocks, resident)),
        name="in_proj",
    )(u, w_in_t)


def _gate_proj_kernel(u_ref, wt_ref, wflt_ref, zg_ref, fl_ref):
    u = u_ref[...]
    zg_ref[...] = _dot_nt(u, wt_ref[...]).astype(BF16)

    @pl.when(pl.program_id(1) == 0)
    def _():
        fl_ref[...] = _dot_nt(u, wflt_ref[...])


def _gate_proj(u, w_hi_t, w_fl_t, *, tm, tn):
    m = u.shape[0]
    n_cols = w_hi_t.shape[0]
    blocks = tm * D_MODEL * 2 + D_MODEL * tn * 2 + D_MODEL * LANES * 2 + tm * tn * 2 + tm * LANES * 4
    resident = tm * tn * 4
    return pl.pallas_call(
        _gate_proj_kernel,
        grid=(m // tm, n_cols // tn),
        in_specs=[
            pl.BlockSpec((tm, D_MODEL), lambda i, j: (i, 0)),
            pl.BlockSpec((tn, D_MODEL), lambda i, j: (j, 0)),
            pl.BlockSpec((LANES, D_MODEL), lambda i, j: (0, 0)),
        ],
        out_specs=[
            pl.BlockSpec((tm, tn), lambda i, j: (i, j)),
            pl.BlockSpec((tm, LANES), lambda i, j: (i, 0)),
        ],
        out_shape=[
            jax.ShapeDtypeStruct((m, n_cols), BF16),
            jax.ShapeDtypeStruct((m, LANES), F32),
        ],
        compiler_params=pltpu.CompilerParams(
            dimension_semantics=("parallel", "arbitrary"),
            vmem_limit_bytes=_vmem_limit(blocks, resident)),
        name="gate_proj",
    )(u, w_hi_t, w_fl_t)


def _gelu_tanh(x):
    c = (2.0 / jnp.pi) ** 0.5
    half = 0.5 * x
    return half + half * jnp.tanh(x * (c + (c * 0.044715) * (x * x)))


def _rnn_kernel(xr_ref, gr_ref, ctx_ref, h0_ref, cw_ref, cb_ref, wax_ref, ba_ref, bx_ref, lam_ref,
                y_ref, hlast_ref, xbuf, a_scr, b_scr, h_scr, carry_scr, *, tt, cb, pitch):
    t = pl.program_id(1)
    nslab = cb // LRU_BS

    @pl.when(t == 0)
    def _():
        xbuf[:, 0:SUBLANES, :] = jnp.broadcast_to(ctx_ref[...][None], (SUBLANES, SUBLANES, cb))
        carry_scr[...] = jnp.broadcast_to(h0_ref[...], (SUBLANES, cb))

    x = xr_ref[...].astype(F32)
    xbuf[:, SUBLANES:SUBLANES + tt, :] = x
    xc = xbuf[:, pl.ds(SUBLANES - 3, tt), :] * cw_ref[0:1, :]
    xc = xc + xbuf[:, pl.ds(SUBLANES - 2, tt), :] * cw_ref[1:2, :]
    xc = xc + xbuf[:, pl.ds(SUBLANES - 1, tt), :] * cw_ref[2:3, :]
    xc = xc + x * cw_ref[3:4, :]
    xc = xc + cb_ref[...]
    xbuf[:, 0:SUBLANES, :] = xbuf[:, tt:tt + SUBLANES, :]

    for n in range(nslab):
        cols = slice(n * LRU_BS, (n + 1) * LRU_BS)
        xn = xc[:, :, cols].reshape(SUBLANES * tt, LRU_BS)
        g = jnp.dot(xn.astype(BF16), wax_ref[n], preferred_element_type=F32)
        r = jax.nn.sigmoid(g[:, :LRU_BS] + ba_ref[:, cols])
        i = jax.nn.sigmoid(g[:, LRU_BS:] + bx_ref[:, cols])
        neg_lam = -lam_ref[:, cols]
        softplus = jnp.maximum(neg_lam, 0.0) + jnp.log1p(jnp.exp(-jnp.abs(neg_lam)))
        a = jnp.exp2((-LRU_C * LOG2E * softplus) * r)
        v = 1.0 - a * a
        bv = jnp.where(v > 0.0, v * lax.rsqrt(v), 0.0) * (i * xn)
        for b in range(SUBLANES):
            a_scr[n, b * pitch:b * pitch + tt, :] = a[b * tt:(b + 1) * tt]
            b_scr[n, b * pitch:b * pitch + tt, :] = bv[b * tt:(b + 1) * tt]

    def step(tl, hs):
        rows = pl.ds(tl, SUBLANES, stride=pitch)
        hs = tuple(a_scr[n, rows, :] * hs[n] + b_scr[n, rows, :] for n in range(nslab))
        for n in range(nslab):
            h_scr[n, rows, :] = hs[n]
        return hs

    hs = tuple(carry_scr[:, n * LRU_BS:(n + 1) * LRU_BS] for n in range(nslab))
    hs = lax.fori_loop(0, tt, step, hs, unroll=8)
    for n in range(nslab):
        cols = slice(n * LRU_BS, (n + 1) * LRU_BS)
        carry_scr[:, cols] = hs[n]
        hlast_ref[:, cols] = hs[n]
        for b in range(SUBLANES):
            gate = _gelu_tanh(gr_ref[b, :, cols].astype(F32))
            y_ref[b, :, cols] = (h_scr[n, b * pitch:b * pitch + tt, :] * gate).astype(BF16)


def _rnn(z3, ctx, h0, conv_w, conv_b, w_ax, b_a, b_x, lam, *, tt, cb):
    b, t_len, _ = z3.shape
    assert b == SUBLANES
    ncb = D_MODEL // cb
    nblk = cb // LRU_BS
    pitch = tt + SUBLANES if (tt // SUBLANES) % 2 == 0 else tt + 2 * SUBLANES
    vec = lambda rows: pl.BlockSpec((rows, cb), lambda ci, ti: (0, ci))
    blocks = 3 * b * tt * cb * 2 + nblk * LRU_BS * 2 * LRU_BS * 2 + b * cb * 4
    resident = b * (tt + SUBLANES) * cb * 4 + 3 * b * pitch * cb * 4 + 6 * b * tt * cb * 4
    return pl.pallas_call(
        functools.partial(_rnn_kernel, tt=tt, cb=cb, pitch=pitch),
        grid=(ncb, t_len // tt),
        in_specs=[
            pl.BlockSpec((b, tt, cb), lambda ci, ti: (0, ti, P_XR * ncb + ci)),
            pl.BlockSpec((b, tt, cb), lambda ci, ti: (0, ti, P_GR * ncb + ci)),
            vec(SUBLANES), vec(1), vec(CONV_W), vec(1),
            pl.BlockSpec((nblk, LRU_BS, 2 * LRU_BS), lambda ci, ti: (ci, 0, 0)),
            vec(1), vec(1), vec(1),
        ],
        out_specs=[
            pl.BlockSpec((b, tt, cb), lambda ci, ti: (0, ti, ci)),
            pl.BlockSpec((b, cb), lambda ci, ti: (0, ci)),
        ],
        out_shape=[
            jax.ShapeDtypeStruct((b, t_len, D_MODEL), BF16),
            jax.ShapeDtypeStruct((b, D_MODEL), F32),
        ],
        scratch_shapes=[
            pltpu.VMEM((b, tt + SUBLANES, cb), F32),
            pltpu.VMEM((nblk, b * pitch, LRU_BS), F32),
            pltpu.VMEM((nblk, b * pitch, LRU_BS), F32),
            pltpu.VMEM((nblk, b * pitch, LRU_BS), F32),
            pltpu.VMEM((b, cb), F32),
        ],
        compiler_params=pltpu.CompilerParams(
            dimension_semantics=("parallel", "arbitrary"),
            vmem_limit_bytes=_vmem_limit(blocks, resident)),
        name="rnn",
    )(z3, z3, ctx, h0, conv_w, conv_b, w_ax, b_a, b_x, lam)


def _forget_kernel(x_ref, fb_ref, nf_ref, *, n_blocks, first_valid):
    lane = lax.broadcasted_iota(jnp.int32, (N_HEADS, LANES), 1)
    fb = fb_ref[...]
    run = jnp.zeros((N_HEADS, 1), F32)
    for k in range(n_blocks):
        x = x_ref[0, :, k * LANES:(k + 1) * LANES] + fb
        lf = jnp.minimum(x, 0.0) - jnp.log1p(jnp.exp(-jnp.abs(x)))
        if k == 0:
            lf = jnp.where(lane >= first_valid, lf, 0.0)
        d = 1
        while d < LANES:
            lf = jnp.where(lane >= d, lf + pltpu.roll(lf, d, axis=1), lf)
            d *= 2
        f = lf + run
        run = f[:, LANES - 1:LANES]
        neg = f * -LOG2E
        if k == 0:
            neg = jnp.where(lane >= first_valid, neg, NEG_BIG)
        nf_ref[0, :, k * LANES:(k + 1) * LANES] = neg


def _forget_prefix(x_all, forget_b):
    b, _, n = x_all.shape
    return pl.pallas_call(
        functools.partial(_forget_kernel, n_blocks=n // LANES, first_valid=LANES - N_META),
        grid=(b,),
        in_specs=[
            pl.BlockSpec((1, N_HEADS, n), lambda bi: (bi, 0, 0)),
            pl.BlockSpec((N_HEADS, 1), lambda bi: (0, 0)),
        ],
        out_specs=pl.BlockSpec((1, N_HEADS, n), lambda bi: (bi, 0, 0)),
        out_shape=jax.ShapeDtypeStruct((b, N_HEADS, n), F32),
        compiler_params=pltpu.CompilerParams(dimension_semantics=("parallel",)),
        name="forget_prefix",
    )(x_all, forget_b)


def _attn_kernel(q_ref, k_ref, v_ref, km_ref, vm_ref, nf_ref, o_ref, vx_scr, vmx_scr, *, tq, nq, hg):
    qi = pl.program_id(2)
    causal = (lax.broadcasted_iota(jnp.int32, (tq, tq), 0)
              >= lax.broadcasted_iota(jnp.int32, (tq, tq), 1))

    @pl.when(qi == 0)
    def _():
        for hh in range(hg):
            cols = slice(hh * HEAD_DIM, (hh + 1) * HEAD_DIM)
            vx_scr[hh, :, 0:HEAD_DIM] = v_ref[0, :, cols]
            vx_scr[hh, :, HEAD_DIM:] = jnp.ones((vx_scr.shape[1], HEAD_DIM), BF16)
            vmx_scr[hh, :, 0:HEAD_DIM] = vm_ref[:, cols]
            vmx_scr[hh, :, HEAD_DIM:] = jnp.ones((LANES, HEAD_DIM), BF16)

    def head(hh, c):
        cols = slice(hh * HEAD_DIM, (hh + 1) * HEAD_DIM)
        q = q_ref[0, :, cols]

        def scores(k_blk, neg_f):
            s = lax.dot_general(q, k_blk, (((1,), (1,)), ((), ())), preferred_element_type=F32)
            return s + neg_f

        lo = c * tq
        ss = [scores(km_ref[:, cols], nf_ref[hh, :, 0:LANES])]
        vs = [vmx_scr[hh]]
        if c > 0:
            ss.append(scores(k_ref[0, 0:lo, cols], nf_ref[hh, :, LANES:LANES + lo]))
            vs.append(vx_scr[hh, 0:lo, :])
        s_diag = scores(k_ref[0, lo:lo + tq, cols], nf_ref[hh, :, LANES + lo:LANES + lo + tq])
        ss.append(jnp.where(causal, s_diag, NEG_BIG))
        vs.append(vx_scr[hh, lo:lo + tq, :])
        m = functools.reduce(jnp.maximum, [jnp.max(s, axis=1, keepdims=True) for s in ss])
        ol = jnp.zeros((tq, 2 * HEAD_DIM), F32)
        for s, v_blk in zip(ss, vs):
            ol = ol + jnp.dot(jnp.exp2(s - m).astype(BF16), v_blk, preferred_element_type=F32)
        o_ref[0, :, cols] = (ol[:, :HEAD_DIM] / ol[:, HEAD_DIM:]).astype(BF16)

    for c in range(nq):
        @pl.when(qi == c)
        def _(c=c):
            for hh in range(hg):
                head(hh, c)


def _attention(z3, zm_pad, neg_f, *, tq, hg):
    b, t_len, _ = z3.shape
    nq = t_len // tq
    ng = N_HEADS // hg
    w = hg * HEAD_DIM
    blocks = (tq * w * 2 * 2 + 2 * t_len * w * 2 + 2 * LANES * w * 2
              + hg * SUBLANES * (LANES + t_len) * 4)
    resident = hg * 4 * tq * (t_len + LANES) * 4 + hg * (t_len + LANES) * 2 * HEAD_DIM * 2
    return pl.pallas_call(
        functools.partial(_attn_kernel, tq=tq, nq=nq, hg=hg),
        grid=(b, ng, nq),
        in_specs=[
            pl.BlockSpec((1, tq, w), lambda bi, gi, qi: (bi, qi, P_Q * ng + gi)),
            pl.BlockSpec((1, t_len, w), lambda bi, gi, qi: (bi, 0, P_K * ng + gi)),
            pl.BlockSpec((1, t_len, w), lambda bi, gi, qi: (bi, 0, P_V * ng + gi)),
            pl.BlockSpec((LANES, w), lambda bi, gi, qi: (0, P_K * ng + gi)),
            pl.BlockSpec((LANES, w), lambda bi, gi, qi: (0, P_V * ng + gi)),
            pl.BlockSpec((hg, 1, LANES + t_len), lambda bi, gi, qi: (bi * ng + gi, 0, 0)),
        ],
        out_specs=pl.BlockSpec((1, tq, w), lambda bi, gi, qi: (bi, qi, gi)),
        out_shape=jax.ShapeDtypeStruct((b, t_len, N_HEADS * HEAD_DIM), BF16),
        scratch_shapes=[
            pltpu.VMEM((hg, t_len, 2 * HEAD_DIM), BF16),
            pltpu.VMEM((hg, LANES, 2 * HEAD_DIM), BF16),
        ],
        compiler_params=pltpu.CompilerParams(
            dimension_semantics=("parallel", "parallel", "arbitrary"),
            vmem_limit_bytes=_vmem_limit(blocks, resident)),
        name="attention",
    )(z3, z3, z3, zm_pad, zm_pad, neg_f)


def _merge_kernel(yr_ref, ya_ref, wr_ref, wa_ref, gr_ref, ga_ref, o_ref):
    pr = jnp.dot(yr_ref[...], wr_ref[...], preferred_element_type=F32)
    pa = jnp.dot(ya_ref[...], wa_ref[...], preferred_element_type=F32)
    merged = (jax.nn.sigmoid(gr_ref[...].astype(F32)) * pr
              + jax.nn.sigmoid(ga_ref[...].astype(F32)) * pa)
    o_ref[...] = merged.astype(BF16)


def _merge(y_rnn, y_attn, w_rnn_out, w_attn_out, zg, *, tm, tn):
    m = y_rnn.shape[0]
    nn = D_MODEL // tn
    blocks = 2 * tm * D_MODEL * 2 + 2 * D_MODEL * tn * 2 + 3 * tm * tn * 2
    resident = 3 * tm * tn * 4
    return pl.pallas_call(
        _merge_kernel,
        grid=(m // tm, nn),
        in_specs=[
            pl.BlockSpec((tm, D_MODEL), lambda i, j: (i, 0)),
            pl.BlockSpec((tm, D_MODEL), lambda i, j: (i, 0)),
            pl.BlockSpec((D_MODEL, tn), lambda i, j: (0, j)),
            pl.BlockSpec((D_MODEL, tn), lambda i, j: (0, j)),
            pl.BlockSpec((tm, tn), lambda i, j: (i, P_GRNN * nn + j)),
            pl.BlockSpec((tm, tn), lambda i, j: (i, P_GATTN * nn + j)),
        ],
        out_specs=pl.BlockSpec((tm, tn), lambda i, j: (i, j)),
        out_shape=jax.ShapeDtypeStruct((m, D_MODEL), BF16),
        compiler_params=pltpu.CompilerParams(
            dimension_semantics=("parallel", "arbitrary"),
            vmem_limit_bytes=_vmem_limit(blocks, resident)),
        name="merge",
    )(y_rnn, y_attn, w_rnn_out, w_attn_out, zg, zg)


def _out_proj_kernel(x_ref, w_ref, h_ref, g_ref, o_ref):
    mix = jnp.dot(x_ref[...], w_ref[...], preferred_element_type=F32)
    o_ref[...] = h_ref[...] + _rms(mix, g_ref[...])


def _out_proj(merged, w_o, h, g_post, *, tm):
    m = merged.shape[0]
    blocks = tm * D_MODEL * (2 + 4 + 4) + D_MODEL * D_MODEL * 2
    resident = tm * D_MODEL * 4
    return pl.pallas_call(
        _out_proj_kernel,
        grid=(m // tm,),
        in_specs=[
            pl.BlockSpec((tm, D_MODEL), lambda i: (i, 0)),
            pl.BlockSpec((D_MODEL, D_MODEL), lambda i: (0, 0)),
            pl.BlockSpec((tm, D_MODEL), lambda i: (i, 0)),
            pl.BlockSpec((1, D_MODEL), lambda i: (0, 0)),
        ],
        out_specs=pl.BlockSpec((tm, D_MODEL), lambda i: (i, 0)),
        out_shape=jax.ShapeDtypeStruct((m, D_MODEL), F32),
        compiler_params=pltpu.CompilerParams(
            dimension_semantics=("parallel",),
            vmem_limit_bytes=_vmem_limit(blocks, resident)),
        name="out_proj",
    )(merged, w_o, h, g_post)


def kernel(x, meta_tokens, norm_g, ffn1_w_gu, ffn1_w_down, w_in, conv_w, conv_b, lru_w_a, lru_b_a,
           lru_w_x, lru_b_x, lru_lambda, forget_b, w_rnn_out, w_attn_out, w_o, ffn2_w_gu,
           ffn2_w_down):
    b, t_len, d = x.shape
    m = b * t_len
    g = [norm_g[0, k:k + 1] for k in range(6)]
    row = lambda v: v[0].reshape(1, -1)

    w_in_t = w_in[0].T.astype(BF16)
    w_fl_t = w_in_t[Z_COLS:Z_COLS + LANES]
    w_hi_t = w_in_t[Z_COLS + N_HEADS:]
    w_ax = jnp.concatenate([lru_w_a[0], lru_w_x[0]], axis=-1).astype(BF16)
    w_gu1, w_d1 = ffn1_w_gu[0].astype(BF16), ffn1_w_down[0].astype(BF16)
    w_gu2, w_d2 = ffn2_w_gu[0].astype(BF16), ffn2_w_down[0].astype(BF16)
    w_r, w_a, w_out = w_rnn_out[0].astype(BF16), w_attn_out[0].astype(BF16), w_o[0].astype(BF16)
    rnn_params = (conv_w[0], row(conv_b), w_ax, row(lru_b_a), row(lru_b_x), row(lru_lambda))

    _, u_m = _ffn(meta_tokens, g[0], w_gu1, w_d1, g[1], g[2], tm=N_META)
    z_m = _in_proj(u_m, w_in_t, tm=N_META, tn=PROJ_TN)
    _, fl_m = _gate_proj(u_m, w_hi_t[:LANES], w_fl_t, tm=N_META, tn=LANES)
    zeros = jnp.zeros((SUBLANES, D_MODEL), F32)
    _, h_meta = _rnn(jnp.broadcast_to(z_m[None], (SUBLANES, N_META, Z_COLS)), zeros, zeros[:1],
                     *rnn_params, tt=N_META, cb=RNN_CB)
    ctx = z_m[N_META - SUBLANES:, :D_MODEL].astype(F32)
    zm_pad = jnp.pad(z_m, ((LANES - N_META, 0), (0, 0)))

    h1, u2 = _ffn(x.reshape(m, d), g[0], w_gu1, w_d1, g[1], g[2], tm=FFN_TM)
    z = _in_proj(u2, w_in_t, tm=PROJ_TM, tn=PROJ_TN)
    zg, fl = _gate_proj(u2, w_hi_t, w_fl_t, tm=PROJ_TM, tn=PROJ_TN)
    z3 = z.reshape(b, t_len, Z_COLS)
    y_rnn, _ = _rnn(z3, ctx, h_meta[:1], *rnn_params, tt=RNN_TT, cb=RNN_CB)

    fl_meta = jnp.pad(fl_m[:, :N_HEADS].T, ((0, 0), (LANES - N_META, 0)))
    fl_real = fl[:, :N_HEADS].reshape(b, t_len, N_HEADS).transpose(0, 2, 1)
    x_all = jnp.concatenate([jnp.broadcast_to(fl_meta[None], (b, N_HEADS, LANES)), fl_real], axis=2)
    neg_f = _forget_prefix(x_all, forget_b[0].reshape(N_HEADS, 1))
    y_attn = _attention(z3, zm_pad, neg_f.reshape(b * N_HEADS, 1, LANES + t_len),
                        tq=ATTN_TQ, hg=ATTN_HEADS_PER_STEP)

    merged = _merge(y_rnn.reshape(m, d), y_attn.reshape(m, d), w_r, w_a, zg, tm=MERGE_TM, tn=MERGE_TN)
    h2 = _out_proj(merged, w_out, h1, g[3], tm=OUT_TM)
    (h3,) = _ffn(h2, g[4], w_gu2, w_d2, g[5], None, tm=FFN_TM)
    return h3.reshape(b, t_len, d)
```

```python
import functools

import jax
import jax.numpy as jnp
from jax import lax
from jax.experimental import pallas as pl
from jax.experimental.pallas import tpu as pltpu

D_MODEL = 2048
N_META = 16
D_FF = 5632
N_HEADS = 16
HEAD_DIM = 128
LRU_BS = 128
LRU_BLOCKS = D_MODEL // LRU_BS
CONV_W = 4
LRU_C = 8.0
EPS = 1e-6
(P_XR, P_GR, P_Q, P_K, P_V) = range(5)
Z_COLS = 5 * D_MODEL
(P_GRNN, P_GATTN) = range(2)
ZG_COLS = 2 * D_MODEL
FFN_TM, FFN_TF = 512, 512
PROJ_TM, PROJ_TN = 1024, 2048
RNN_TT, RNN_CB = 256, 512
ATTN_TQ, ATTN_HEADS_PER_STEP = 512, 8
MERGE_TM, MERGE_TN = 1024, 512
OUT_TM = 512

LANES = 128
SUBLANES = 8
VMEM_BYTES_V7X = 64 * 1024 * 1024
NEG_BIG = -1e30
LOG2E = 1.4426950408889634
QK_SCALE_LOG2 = HEAD_DIM ** -0.5 * LOG2E

F32 = jnp.float32
BF16 = jnp.bfloat16


def _vmem_limit(pipelined_bytes, resident_bytes):
    return min(2 * pipelined_bytes + resident_bytes + (4 << 20), VMEM_BYTES_V7X - (4 << 20))


def _rms(x, g):
    return (x * lax.rsqrt(jnp.mean(x * x, axis=-1, keepdims=True) + EPS)) * g


def _ffn_kernel(h_ref, gpre_ref, wg_ref, wu_ref, wd_ref, gpost_ref, *rest, with_next):
    if with_next:
        gnext_ref, out_ref, unext_ref, u_scr = rest
    else:
        out_ref, u_scr = rest
    j = pl.program_id(1)

    @pl.when(j == 0)
    def _():
        u_scr[...] = _rms(h_ref[...], gpre_ref[...]).astype(BF16)
        out_ref[...] = jnp.zeros_like(out_ref)

    u = u_scr[...]
    gate = jnp.dot(u, wg_ref[...], preferred_element_type=F32)
    up = jnp.dot(u, wu_ref[...], preferred_element_type=F32)
    act = ((gate * jax.nn.sigmoid(gate)) * up).astype(BF16)
    out_ref[...] += jnp.dot(act, wd_ref[...], preferred_element_type=F32)

    @pl.when(j == pl.num_programs(1) - 1)
    def _():
        h_new = h_ref[...] + 0.5 * _rms(out_ref[...], gpost_ref[...])
        out_ref[...] = h_new
        if with_next:
            unext_ref[...] = _rms(h_new, gnext_ref[...]).astype(BF16)


def _ffn(h, g_pre, w_gu, w_down, g_post, g_next, *, tm):
    m = h.shape[0]
    tf = FFN_TF
    nf = D_FF // tf
    with_next = g_next is not None
    row_tile = pl.BlockSpec((tm, D_MODEL), lambda i, j: (i, 0))
    gain = pl.BlockSpec((1, D_MODEL), lambda i, j: (0, 0))
    blocks = tm * D_MODEL * (4 + 4 + 2 * with_next) + 3 * D_MODEL * tf * 2
    resident = tm * D_MODEL * (2 + 4) + tm * tf * (4 + 4 + 2)
    return pl.pallas_call(
        functools.partial(_ffn_kernel, with_next=with_next),
        grid=(m // tm, nf),
        in_specs=[
            row_tile,
            gain,
            pl.BlockSpec((D_MODEL, tf), lambda i, j: (0, j)),
            pl.BlockSpec((D_MODEL, tf), lambda i, j: (0, j + nf)),
            pl.BlockSpec((tf, D_MODEL), lambda i, j: (j, 0)),
            gain,
        ] + [gain] * with_next,
        out_specs=[row_tile] * (1 + with_next),
        out_shape=[jax.ShapeDtypeStruct((m, D_MODEL), F32)]
        + [jax.ShapeDtypeStruct((m, D_MODEL), BF16)] * with_next,
        scratch_shapes=[pltpu.VMEM((tm, D_MODEL), BF16)],
        compiler_params=pltpu.CompilerParams(
            dimension_semantics=("parallel", "arbitrary"),
            vmem_limit_bytes=_vmem_limit(blocks, resident)),
        name="ffn",
    )(h, g_pre, w_gu, w_gu, w_down, g_post, *([g_next] * with_next))


def _dot_nt(x, wt):
    return lax.dot_general(x, wt, (((1,), (1,)), ((), ())), preferred_element_type=F32)


def _in_proj_kernel(u_ref, wt_ref, z_ref, *, tn):
    j = pl.program_id(0)
    is_q = (j >= P_Q * D_MODEL // tn) & (j < (P_Q + 1) * D_MODEL // tn)
    factor = jnp.where(is_q, QK_SCALE_LOG2, 1.0)
    z_ref[...] = (_dot_nt(u_ref[...], wt_ref[...]) * factor).astype(BF16)


def _in_proj(u, w_in_t, *, tm, tn):
    m = u.shape[0]
    blocks = tm * D_MODEL * 2 + D_MODEL * tn * 2 + tm * tn * 2
    resident = tm * tn * 4
    return pl.pallas_call(
        functools.partial(_in_proj_kernel, tn=tn),
        grid=(Z_COLS // tn, m // tm),
        in_specs=[
            pl.BlockSpec((tm, D_MODEL), lambda j, i: (i, 0)),
            pl.BlockSpec((tn, D_MODEL), lambda j, i: (j, 0)),
        ],
        out_specs=pl.BlockSpec((tm, tn), lambda j, i: (i, j)),
        out_shape=jax.ShapeDtypeStruct((m, Z_COLS), BF16),
        compiler_params=pltpu.CompilerParams(
            dimension_semantics=("parallel", "parallel"),
            vmem_limit_bytes=_vmem_limit(blocks, resident)),
        name="in_proj",
    )(u, w_in_t)


def _gate_proj_kernel(u_ref, wt_ref, wflt_ref, zg_ref, fl_ref):
    u = u_ref[...]
    zg_ref[...] = _dot_nt(u, wt_ref[...]).astype(BF16)

    @pl.when(pl.program_id(1) == 0)
    def _():
        fl_ref[...] = _dot_nt(u, wflt_ref[...])


def _gate_proj(u, w_hi_t, w_fl_t, *, tm, tn):
    m = u.shape[0]
    n_cols = w_hi_t.shape[0]
    blocks = tm * D_MODEL * 2 + D_MODEL * tn * 2 + D_MODEL * LANES * 2 + tm * tn * 2 + tm * LANES * 4
    resident = tm * tn * 4
    return pl.pallas_call(
        _gate_proj_kernel,
        grid=(m // tm, n_cols // tn),
        in_specs=[
            pl.BlockSpec((tm, D_MODEL), lambda i, j: (i, 0)),
            pl.BlockSpec((tn, D_MODEL), lambda i, j: (j, 0)),
            pl.BlockSpec((LANES, D_MODEL), lambda i, j: (0, 0)),
        ],
        out_specs=[
            pl.BlockSpec((tm, tn), lambda i, j: (i, j)),
            pl.BlockSpec((tm, LANES), lambda i, j: (i, 0)),
        ],
        out_shape=[
            jax.ShapeDtypeStruct((m, n_cols), BF16),
            jax.ShapeDtypeStruct((m, LANES), F32),
        ],
        compiler_params=pltpu.CompilerParams(
            dimension_semantics=("parallel", "arbitrary"),
            vmem_limit_bytes=_vmem_limit(blocks, resident)),
        name="gate_proj",
    )(u, w_hi_t, w_fl_t)


def _gelu_tanh(x):
    c = (2.0 / jnp.pi) ** 0.5
    half = 0.5 * x
    return half + half * jnp.tanh(x * (c + (c * 0.044715) * (x * x)))


def _rnn_kernel(xr_ref, gr_ref, ctx_ref, h0_ref, cw_ref, cb_ref, wax_ref, ba_ref, bx_ref, lam_ref,
                y_ref, hlast_ref, xbuf, a_scr, b_scr, h_scr, carry_scr, *, tt, cb, pitch):
    t = pl.program_id(1)
    nslab = cb // LRU_BS

    @pl.when(t == 0)
    def _():
        xbuf[:, 0:SUBLANES, :] = jnp.broadcast_to(ctx_ref[...][None], (SUBLANES, SUBLANES, cb))
        carry_scr[...] = jnp.broadcast_to(h0_ref[...], (SUBLANES, cb))

    x = xr_ref[...].astype(F32)
    xbuf[:, SUBLANES:SUBLANES + tt, :] = x
    xc = xbuf[:, pl.ds(SUBLANES - 3, tt), :] * cw_ref[0:1, :]
    xc = xc + xbuf[:, pl.ds(SUBLANES - 2, tt), :] * cw_ref[1:2, :]
    xc = xc + xbuf[:, pl.ds(SUBLANES - 1, tt), :] * cw_ref[2:3, :]
    xc = xc + x * cw_ref[3:4, :]
    xc = xc + cb_ref[...]
    xbuf[:, 0:SUBLANES, :] = xbuf[:, tt:tt + SUBLANES, :]

    for n in range(nslab):
        cols = slice(n * LRU_BS, (n + 1) * LRU_BS)
        xn = xc[:, :, cols].reshape(SUBLANES * tt, LRU_BS)
        g = jnp.dot(xn.astype(BF16), wax_ref[n], preferred_element_type=F32)
        r = jax.nn.sigmoid(g[:, :LRU_BS] + ba_ref[:, cols])
        i = jax.nn.sigmoid(g[:, LRU_BS:] + bx_ref[:, cols])
        neg_lam = -lam_ref[:, cols]
        softplus = jnp.maximum(neg_lam, 0.0) + jnp.log1p(jnp.exp(-jnp.abs(neg_lam)))
        a = jnp.exp2((-LRU_C * LOG2E * softplus) * r)
        v = 1.0 - a * a
        bv = jnp.where(v > 0.0, v * lax.rsqrt(v), 0.0) * (i * xn)
        for b in range(SUBLANES):
            a_scr[n, b * pitch:b * pitch + tt, :] = a[b * tt:(b + 1) * tt]
            b_scr[n, b * pitch:b * pitch + tt, :] = bv[b * tt:(b + 1) * tt]

    def step(tl, hs):
        rows = pl.ds(tl, SUBLANES, stride=pitch)
        hs = tuple(a_scr[n, rows, :] * hs[n] + b_scr[n, rows, :] for n in range(nslab))
        for n in range(nslab):
            h_scr[n, rows, :] = hs[n]
        return hs

    hs = tuple(carry_scr[:, n * LRU_BS:(n + 1) * LRU_BS] for n in range(nslab))
    hs = lax.fori_loop(0, tt, step, hs, unroll=8)
    for n in range(nslab):
        cols = slice(n * LRU_BS, (n + 1) * LRU_BS)
        carry_scr[:, cols] = hs[n]
        hlast_ref[:, cols] = hs[n]
        for b in range(SUBLANES):
            gate = _gelu_tanh(gr_ref[b, :, cols].astype(F32))
            y_ref[b, :, cols] = (h_scr[n, b * pitch:b * pitch + tt, :] * gate).astype(BF16)


def _rnn(z3, ctx, h0, conv_w, conv_b, w_ax, b_a, b_x, lam, *, tt, cb):
    b, t_len, _ = z3.shape
    assert b == SUBLANES
    ncb = D_MODEL // cb
    nblk = cb // LRU_BS
    pitch = tt + SUBLANES if (tt // SUBLANES) % 2 == 0 else tt + 2 * SUBLANES
    vec = lambda rows: pl.BlockSpec((rows, cb), lambda ci, ti: (0, ci))
    blocks = 3 * b * tt * cb * 2 + nblk * LRU_BS * 2 * LRU_BS * 2 + b * cb * 4
    resident = b * (tt + SUBLANES) * cb * 4 + 3 * b * pitch * cb * 4 + 6 * b * tt * cb * 4
    return pl.pallas_call(
        functools.partial(_rnn_kernel, tt=tt, cb=cb, pitch=pitch),
        grid=(ncb, t_len // tt),
        in_specs=[
            pl.BlockSpec((b, tt, cb), lambda ci, ti: (0, ti, P_XR * ncb + ci)),
            pl.BlockSpec((b, tt, cb), lambda ci, ti: (0, ti, P_GR * ncb + ci)),
            vec(SUBLANES), vec(1), vec(CONV_W), vec(1),
            pl.BlockSpec((nblk, LRU_BS, 2 * LRU_BS), lambda ci, ti: (ci, 0, 0)),
            vec(1), vec(1), vec(1),
        ],
        out_specs=[
            pl.BlockSpec((b, tt, cb), lambda ci, ti: (0, ti, ci)),
            pl.BlockSpec((b, cb), lambda ci, ti: (0, ci)),
        ],
        out_shape=[
            jax.ShapeDtypeStruct((b, t_len, D_MODEL), BF16),
            jax.ShapeDtypeStruct((b, D_MODEL), F32),
        ],
        scratch_shapes=[
            pltpu.VMEM((b, tt + SUBLANES, cb), F32),
            pltpu.VMEM((nblk, b * pitch, LRU_BS), F32),
            pltpu.VMEM((nblk, b * pitch, LRU_BS), F32),
            pltpu.VMEM((nblk, b * pitch, LRU_BS), F32),
            pltpu.VMEM((b, cb), F32),
        ],
        compiler_params=pltpu.CompilerParams(
            dimension_semantics=("parallel", "arbitrary"),
            vmem_limit_bytes=_vmem_limit(blocks, resident)),
        name="rnn",
    )(z3, z3, ctx, h0, conv_w, conv_b, w_ax, b_a, b_x, lam)


def _forget_kernel(x_ref, fb_ref, nf_ref, *, n_blocks, first_valid):
    lane = lax.broadcasted_iota(jnp.int32, (N_HEADS, LANES), 1)
    fb = fb_ref[...]
    run = jnp.zeros((N_HEADS, 1), F32)
    for k in range(n_blocks):
        x = x_ref[0, :, k * LANES:(k + 1) * LANES] + fb
        lf = jnp.minimum(x, 0.0) - jnp.log1p(jnp.exp(-jnp.abs(x)))
        if k == 0:
            lf = jnp.where(lane >= first_valid, lf, 0.0)
        d = 1
        while d < LANES:
            lf = jnp.where(lane >= d, lf + pltpu.roll(lf, d, axis=1), lf)
            d *= 2
        f = lf + run
        run = f[:, LANES - 1:LANES]
        neg = f * -LOG2E
        if k == 0:
            neg = jnp.where(lane >= first_valid, neg, NEG_BIG)
        nf_ref[0, :, k * LANES:(k + 1) * LANES] = neg


def _forget_prefix(x_all, forget_b):
    b, _, n = x_all.shape
    return pl.pallas_call(
        functools.partial(_forget_kernel, n_blocks=n // LANES, first_valid=LANES - N_META),
        grid=(b,),
        in_specs=[
            pl.BlockSpec((1, N_HEADS, n), lambda bi: (bi, 0, 0)),
            pl.BlockSpec((N_HEADS, 1), lambda bi: (0, 0)),
        ],
        out_specs=pl.BlockSpec((1, N_HEADS, n), lambda bi: (bi, 0, 0)),
        out_shape=jax.ShapeDtypeStruct((b, N_HEADS, n), F32),
        compiler_params=pltpu.CompilerParams(dimension_semantics=("parallel",)),
        name="forget_prefix",
    )(x_all, forget_b)


def _attn_kernel(q_ref, k_ref, v_ref, km_ref, vm_ref, nf_ref, o_ref, vx_scr, vmx_scr, *, tq, nq, hg):
    qi = pl.program_id(2)
    causal = (lax.broadcasted_iota(jnp.int32, (tq, tq), 0)
              >= lax.broadcasted_iota(jnp.int32, (tq, tq), 1))

    @pl.when(qi == 0)
    def _():
        for hh in range(hg):
            cols = slice(hh * HEAD_DIM, (hh + 1) * HEAD_DIM)
            vx_scr[hh, :, 0:HEAD_DIM] = v_ref[0, :, cols]
            vx_scr[hh, :, HEAD_DIM:] = jnp.ones((vx_scr.shape[1], HEAD_DIM), BF16)
            vmx_scr[hh, :, 0:HEAD_DIM] = vm_ref[:, cols]
            vmx_scr[hh, :, HEAD_DIM:] = jnp.ones((LANES, HEAD_DIM), BF16)

    def head(hh, c):
        cols = slice(hh * HEAD_DIM, (hh + 1) * HEAD_DIM)
        q = q_ref[0, :, cols]

        def scores(k_blk, neg_f):
            s = lax.dot_general(q, k_blk, (((1,), (1,)), ((), ())), preferred_element_type=F32)
            return s + neg_f

        lo = c * tq
        ss = [scores(km_ref[:, cols], nf_ref[hh, :, 0:LANES])]
        vs = [vmx_scr[hh]]
        if c > 0:
            ss.append(scores(k_ref[0, 0:lo, cols], nf_ref[hh, :, LANES:LANES + lo]))
            vs.append(vx_scr[hh, 0:lo, :])
        s_diag = scores(k_ref[0, lo:lo + tq, cols], nf_ref[hh, :, LANES + lo:LANES + lo + tq])
        ss.append(jnp.where(causal, s_diag, NEG_BIG))
        vs.append(vx_scr[hh, lo:lo + tq, :])
        m = functools.reduce(jnp.maximum, [jnp.max(s, axis=1, keepdims=True) for s in ss])
        ol = jnp.zeros((tq, 2 * HEAD_DIM), F32)
        for s, v_blk in zip(ss, vs):
            ol = ol + jnp.dot(jnp.exp2(s - m).astype(BF16), v_blk, preferred_element_type=F32)
        o_ref[0, :, cols] = (ol[:, :HEAD_DIM] / ol[:, HEAD_DIM:]).astype(BF16)

    for c in range(nq):
        @pl.when(qi == c)
        def _(c=c):
            for hh in range(hg):
                head(hh, c)


def _attention(z3, zm_pad, neg_f, *, tq, hg):
    b, t_len, _ = z3.shape
    nq = t_len // tq
    ng = N_HEADS // hg
    w = hg * HEAD_DIM
    blocks = (tq * w * 2 * 2 + 2 * t_len * w * 2 + 2 * LANES * w * 2
              + hg * SUBLANES * (LANES + t_len) * 4)
    resident = hg * 4 * tq * (t_len + LANES) * 4 + hg * (t_len + LANES) * 2 * HEAD_DIM * 2
    return pl.pallas_call(
        functools.partial(_attn_kernel, tq=tq, nq=nq, hg=hg),
        grid=(b, ng, nq),
        in_specs=[
            pl.BlockSpec((1, tq, w), lambda bi, gi, qi: (bi, qi, P_Q * ng + gi)),
            pl.BlockSpec((1, t_len, w), lambda bi, gi, qi: (bi, 0, P_K * ng + gi)),
            pl.BlockSpec((1, t_len, w), lambda bi, gi, qi: (bi, 0, P_V * ng + gi)),
            pl.BlockSpec((LANES, w), lambda bi, gi, qi: (0, P_K * ng + gi)),
            pl.BlockSpec((LANES, w), lambda bi, gi, qi: (0, P_V * ng + gi)),
            pl.BlockSpec((hg, 1, LANES + t_len), lambda bi, gi, qi: (bi * ng + gi, 0, 0)),
        ],
        out_specs=pl.BlockSpec((1, tq, w), lambda bi, gi, qi: (bi, qi, gi)),
        out_shape=jax.ShapeDtypeStruct((b, t_len, N_HEADS * HEAD_DIM), BF16),
        scratch_shapes=[
            pltpu.VMEM((hg, t_len, 2 * HEAD_DIM), BF16),
            pltpu.VMEM((hg, LANES, 2 * HEAD_DIM), BF16),
        ],
        compiler_params=pltpu.CompilerParams(
            dimension_semantics=("parallel", "parallel", "arbitrary"),
            vmem_limit_bytes=_vmem_limit(blocks, resident)),
        name="attention",
    )(z3, z3, z3, zm_pad, zm_pad, neg_f)


def _merge_kernel(yr_ref, ya_ref, wr_ref, wa_ref, gr_ref, ga_ref, o_ref):
    pr = jnp.dot(yr_ref[...], wr_ref[...], preferred_element_type=F32)
    pa = jnp.dot(ya_ref[...], wa_ref[...], preferred_element_type=F32)
    merged = (jax.nn.sigmoid(gr_ref[...].astype(F32)) * pr
              + jax.nn.sigmoid(ga_ref[...].astype(F32)) * pa)
    o_ref[...] = merged.astype(BF16)


def _merge(y_rnn, y_attn, w_rnn_out, w_attn_out, zg, *, tm, tn):
    m = y_rnn.shape[0]
    nn = D_MODEL // tn
    blocks = 2 * tm * D_MODEL * 2 + 2 * D_MODEL * tn * 2 + 3 * tm * tn * 2
    resident = 3 * tm * tn * 4
    return pl.pallas_call(
        _merge_kernel,
        grid=(m // tm, nn),
        in_specs=[
            pl.BlockSpec((tm, D_MODEL), lambda i, j: (i, 0)),
            pl.BlockSpec((tm, D_MODEL), lambda i, j: (i, 0)),
            pl.BlockSpec((D_MODEL, tn), lambda i, j: (0, j)),
            pl.BlockSpec((D_MODEL, tn), lambda i, j: (0, j)),
            pl.BlockSpec((tm, tn), lambda i, j: (i, P_GRNN * nn + j)),
            pl.BlockSpec((tm, tn), lambda i, j: (i, P_GATTN * nn + j)),
        ],
        out_specs=pl.BlockSpec((tm, tn), lambda i, j: (i, j)),
        out_shape=jax.ShapeDtypeStruct((m, D_MODEL), BF16),
        compiler_params=pltpu.CompilerParams(
            dimension_semantics=("parallel", "arbitrary"),
            vmem_limit_bytes=_vmem_limit(blocks, resident)),
        name="merge",
    )(y_rnn, y_attn, w_rnn_out, w_attn_out, zg, zg)


def _out_proj_kernel(x_ref, w_ref, h_ref, g_ref, o_ref):
    mix = jnp.dot(x_ref[...], w_ref[...], preferred_element_type=F32)
    o_ref[...] = h_ref[...] + _rms(mix, g_ref[...])


def _out_proj(merged, w_o, h, g_post, *, tm):
    m = merged.shape[0]
    blocks = tm * D_MODEL * (2 + 4 + 4) + D_MODEL * D_MODEL * 2
    resident = tm * D_MODEL * 4
    return pl.pallas_call(
        _out_proj_kernel,
        grid=(m // tm,),
        in_specs=[
            pl.BlockSpec((tm, D_MODEL), lambda i: (i, 0)),
            pl.BlockSpec((D_MODEL, D_MODEL), lambda i: (0, 0)),
            pl.BlockSpec((tm, D_MODEL), lambda i: (i, 0)),
            pl.BlockSpec((1, D_MODEL), lambda i: (0, 0)),
        ],
        out_specs=pl.BlockSpec((tm, D_MODEL), lambda i: (i, 0)),
        out_shape=jax.ShapeDtypeStruct((m, D_MODEL), F32),
        compiler_params=pltpu.CompilerParams(
            dimension_semantics=("parallel",),
            vmem_limit_bytes=_vmem_limit(blocks, resident)),
        name="out_proj",
    )(merged, w_o, h, g_post)


def kernel(x, meta_tokens, norm_g, ffn1_w_gu, ffn1_w_down, w_in, conv_w, conv_b, lru_w_a, lru_b_a,
           lru_w_x, lru_b_x, lru_lambda, forget_b, w_rnn_out, w_attn_out, w_o, ffn2_w_gu,
           ffn2_w_down):
    b, t_len, d = x.shape
    m = b * t_len
    g = [norm_g[0, k:k + 1] for k in range(6)]
    row = lambda v: v[0].reshape(1, -1)

    w_in_t = w_in[0].T.astype(BF16)
    w_fl_t = w_in_t[Z_COLS:Z_COLS + LANES]
    w_hi_t = w_in_t[Z_COLS + N_HEADS:]
    w_ax = jnp.concatenate([lru_w_a[0], lru_w_x[0]], axis=-1).astype(BF16)
    w_gu1, w_d1 = ffn1_w_gu[0].astype(BF16), ffn1_w_down[0].astype(BF16)
    w_gu2, w_d2 = ffn2_w_gu[0].astype(BF16), ffn2_w_down[0].astype(BF16)
    w_r, w_a, w_out = w_rnn_out[0].astype(BF16), w_attn_out[0].astype(BF16), w_o[0].astype(BF16)
    rnn_params = (conv_w[0], row(conv_b), w_ax, row(lru_b_a), row(lru_b_x), row(lru_lambda))

    _, u_m = _ffn(meta_tokens, g[0], w_gu1, w_d1, g[1], g[2], tm=N_META)
    z_m = _in_proj(u_m, w_in_t, tm=N_META, tn=PROJ_TN)
    _, fl_m = _gate_proj(u_m, w_hi_t[:LANES], w_fl_t, tm=N_META, tn=LANES)
    zeros = jnp.zeros((SUBLANES, D_MODEL), F32)
    _, h_meta = _rnn(jnp.broadcast_to(z_m[None], (SUBLANES, N_META, Z_COLS)), zeros, zeros[:1],
                     *rnn_params, tt=N_META, cb=RNN_CB)
    ctx = z_m[N_META - SUBLANES:, :D_MODEL].astype(F32)
    zm_pad = jnp.pad(z_m, ((LANES - N_META, 0), (0, 0)))

    h1, u2 = _ffn(x.reshape(m, d), g[0], w_gu1, w_d1, g[1], g[2], tm=FFN_TM)
    z = _in_proj(u2, w_in_t, tm=PROJ_TM, tn=PROJ_TN)
    zg, fl = _gate_proj(u2, w_hi_t, w_fl_t, tm=PROJ_TM, tn=PROJ_TN)
    z3 = z.reshape(b, t_len, Z_COLS)
    y_rnn, _ = _rnn(z3, ctx, h_meta[:1], *rnn_params, tt=RNN_TT, cb=RNN_CB)

    fl_meta = jnp.pad(fl_m[:, :N_HEADS].T, ((0, 0), (LANES - N_META, 0)))
    fl_real = fl[:, :N_HEADS].reshape(b, t_len, N_HEADS).transpose(0, 2, 1)
    x_all = jnp.concatenate([jnp.broadcast_to(fl_meta[None], (b, N_HEADS, LANES)), fl_real], axis=2)
    neg_f = _forget_prefix(x_all, forget_b[0].reshape(N_HEADS, 1))
    y_attn = _attention(z3, zm_pad, neg_f.reshape(b * N_HEADS, 1, LANES + t_len),
                        tq=ATTN_TQ, hg=ATTN_HEADS_PER_STEP)

    merged = _merge(y_rnn.reshape(m, d), y_attn.reshape(m, d), w_r, w_a, zg, tm=MERGE_TM, tn=MERGE_TN)
    h2 = _out_proj(merged, w_out, h1, g[3], tm=OUT_TM)
    (h3,) = _ffn(h2, g[4], w_gu2, w_d2, g[5], None, tm=FFN_TM)
    return h3.reshape(b, t_len, d)
```

```python
import functools

import jax
import jax.numpy as jnp
from jax import lax
from jax.experimental import pallas as pl
from jax.experimental.pallas import tpu as pltpu

D_MODEL = 2048
N_META = 16
D_FF = 5632
N_HEADS = 16
HEAD_DIM = 128
LRU_BS = 128
LRU_BLOCKS = D_MODEL // LRU_BS
CONV_W = 4
LRU_C = 8.0
EPS = 1e-6
(P_XR, P_GR, P_Q, P_K, P_V) = range(5)
Z_COLS = 5 * D_MODEL
(P_GRNN, P_GATTN) = range(2)
ZG_COLS = 2 * D_MODEL
FFN_TM, FFN_TF = 512, 512
PROJ_TM, PROJ_TN = 1024, 2048
RNN_TT, RNN_CB = 256, 512
ATTN_TQ, ATTN_HEADS_PER_STEP = 512, 4
MERGE_TM, MERGE_TN = 1024, 512
OUT_TM = 512

LANES = 128
SUBLANES = 8
VMEM_BYTES_V7X = 64 * 1024 * 1024
NEG_BIG = -1e30
LOG2E = 1.4426950408889634
QK_SCALE_LOG2 = HEAD_DIM ** -0.5 * LOG2E

F32 = jnp.float32
BF16 = jnp.bfloat16


def _vmem_limit(pipelined_bytes, resident_bytes):
    return min(2 * pipelined_bytes + resident_bytes + (4 << 20), VMEM_BYTES_V7X - (4 << 20))


def _rms(x, g):
    return (x * lax.rsqrt(jnp.mean(x * x, axis=-1, keepdims=True) + EPS)) * g


def _ffn_kernel(h_ref, gpre_ref, wg_ref, wu_ref, wd_ref, gpost_ref, *rest, with_next):
    if with_next:
        gnext_ref, out_ref, unext_ref, u_scr = rest
    else:
        out_ref, u_scr = rest
    j = pl.program_id(1)

    @pl.when(j == 0)
    def _():
        u_scr[...] = _rms(h_ref[...], gpre_ref[...]).astype(BF16)
        out_ref[...] = jnp.zeros_like(out_ref)

    u = u_scr[...]
    gate = jnp.dot(u, wg_ref[...], preferred_element_type=F32)
    up = jnp.dot(u, wu_ref[...], preferred_element_type=F32)
    act = ((gate * jax.nn.sigmoid(gate)) * up).astype(BF16)
    out_ref[...] += jnp.dot(act, wd_ref[...], preferred_element_type=F32)

    @pl.when(j == pl.num_programs(1) - 1)
    def _():
        h_new = h_ref[...] + _rms(out_ref[...], 0.5 * gpost_ref[...])
        out_ref[...] = h_new
        if with_next:
            unext_ref[...] = _rms(h_new, gnext_ref[...]).astype(BF16)


def _ffn(h, g_pre, w_gu, w_down, g_post, g_next, *, tm):
    m = h.shape[0]
    tf = FFN_TF
    nf = D_FF // tf
    with_next = g_next is not None
    row_tile = pl.BlockSpec((tm, D_MODEL), lambda i, j: (i, 0))
    gain = pl.BlockSpec((1, D_MODEL), lambda i, j: (0, 0))
    blocks = tm * D_MODEL * (4 + 4 + 2 * with_next) + 3 * D_MODEL * tf * 2
    resident = tm * D_MODEL * (2 + 4) + tm * tf * (4 + 4 + 2)
    return pl.pallas_call(
        functools.partial(_ffn_kernel, with_next=with_next),
        grid=(m // tm, nf),
        in_specs=[
            row_tile,
            gain,
            pl.BlockSpec((D_MODEL, tf), lambda i, j: (0, j)),
            pl.BlockSpec((D_MODEL, tf), lambda i, j: (0, j + nf)),
            pl.BlockSpec((tf, D_MODEL), lambda i, j: (j, 0)),
            gain,
        ] + [gain] * with_next,
        out_specs=[row_tile] * (1 + with_next),
        out_shape=[jax.ShapeDtypeStruct((m, D_MODEL), F32)]
        + [jax.ShapeDtypeStruct((m, D_MODEL), BF16)] * with_next,
        scratch_shapes=[pltpu.VMEM((tm, D_MODEL), BF16)],
        compiler_params=pltpu.CompilerParams(
            dimension_semantics=("parallel", "arbitrary"),
            vmem_limit_bytes=_vmem_limit(blocks, resident)),
        name="ffn",
    )(h, g_pre, w_gu, w_gu, w_down, g_post, *([g_next] * with_next))


def _dot_nt(x, wt):
    return lax.dot_general(x, wt, (((1,), (1,)), ((), ())), preferred_element_type=F32)


def _in_proj_kernel(u_ref, wt_ref, z_ref, *, tn):
    j = pl.program_id(0)
    is_q = (j >= P_Q * D_MODEL // tn) & (j < (P_Q + 1) * D_MODEL // tn)
    factor = jnp.where(is_q, QK_SCALE_LOG2, 1.0)
    z_ref[...] = (_dot_nt(u_ref[...], wt_ref[...]) * factor).astype(BF16)


def _in_proj(u, w_in_t, *, tm, tn):
    m = u.shape[0]
    blocks = tm * D_MODEL * 2 + D_MODEL * tn * 2 + tm * tn * 2
    resident = tm * tn * 4
    return pl.pallas_call(
        functools.partial(_in_proj_kernel, tn=tn),
        grid=(Z_COLS // tn, m // tm),
        in_specs=[
            pl.BlockSpec((tm, D_MODEL), lambda j, i: (i, 0)),
            pl.BlockSpec((tn, D_MODEL), lambda j, i: (j, 0)),
        ],
        out_specs=pl.BlockSpec((tm, tn), lambda j, i: (i, j)),
        out_shape=jax.ShapeDtypeStruct((m, Z_COLS), BF16),
        compiler_params=pltpu.CompilerParams(
            dimension_semantics=("parallel", "parallel"),
            vmem_limit_bytes=_vmem_limit(blocks, resident)),
        name="in_proj",
    )(u, w_in_t)


def _gate_proj_kernel(u_ref, wt_ref, wflt_ref, zg_ref, fl_ref):
    u = u_ref[...]
    zg_ref[...] = _dot_nt(u, wt_ref[...]).astype(BF16)

    @pl.when(pl.program_id(1) == 0)
    def _():
        fl_ref[...] = _dot_nt(u, wflt_ref[...])


def _gate_proj(u, w_hi_t, w_fl_t, *, tm, tn):
    m = u.shape[0]
    n_cols = w_hi_t.shape[0]
    blocks = tm * D_MODEL * 2 + D_MODEL * tn * 2 + D_MODEL * LANES * 2 + tm * tn * 2 + tm * LANES * 4
    resident = tm * tn * 4
    return pl.pallas_call(
        _gate_proj_kernel,
        grid=(m // tm, n_cols // tn),
        in_specs=[
            pl.BlockSpec((tm, D_MODEL), lambda i, j: (i, 0)),
            pl.BlockSpec((tn, D_MODEL), lambda i, j: (j, 0)),
            pl.BlockSpec((LANES, D_MODEL), lambda i, j: (0, 0)),
        ],
        out_specs=[
            pl.BlockSpec((tm, tn), lambda i, j: (i, j)),
            pl.BlockSpec((tm, LANES), lambda i, j: (i, 0)),
        ],
        out_shape=[
            jax.ShapeDtypeStruct((m, n_cols), BF16),
            jax.ShapeDtypeStruct((m, LANES), F32),
        ],
        compiler_params=pltpu.CompilerParams(
            dimension_semantics=("parallel", "arbitrary"),
            vmem_limit_bytes=_vmem_limit(blocks, resident)),
        name="gate_proj",
    )(u, w_hi_t, w_fl_t)


def _gelu_tanh(x):
    c = (2.0 / jnp.pi) ** 0.5
    half = 0.5 * x
    return half + half * jnp.tanh(x * (c + (c * 0.044715) * (x * x)))


def _rnn_kernel(xr_ref, gr_ref, ctx_ref, h0_ref, cw_ref, cb_ref, wax_ref, ba_ref, bx_ref, lam_ref,
                y_ref, hlast_ref, xbuf, a_scr, b_scr, h_scr, carry_scr, *, tt, cb, pitch):
    t = pl.program_id(1)
    nslab = cb // LRU_BS

    @pl.when(t == 0)
    def _():
        xbuf[:, 0:SUBLANES, :] = jnp.broadcast_to(ctx_ref[...][None], (SUBLANES, SUBLANES, cb))
        carry_scr[...] = jnp.broadcast_to(h0_ref[...], (SUBLANES, cb))

    xb = xr_ref[...]
    x = xb.astype(F32)
    xbuf[:, SUBLANES:, :] = x[:, 0:SUBLANES, :]
    shift = (lax.broadcasted_iota(jnp.int32, (tt, tt), 0)
             - lax.broadcasted_iota(jnp.int32, (tt, tt), 1))
    select = jnp.concatenate([(shift == k).astype(BF16) for k in (3, 2, 1)], axis=0)
    heads = [xbuf[:, pl.ds(SUBLANES - k, SUBLANES), :] for k in (3, 2, 1)]
    shifted = [[], [], []]
    for b in range(SUBLANES):
        body = jnp.dot(select, xb[b], preferred_element_type=F32)
        for s in range(3):
            shifted[s].append(jnp.concatenate(
                [heads[s][b], body[s * tt + SUBLANES:(s + 1) * tt]], axis=0))
    xc = jnp.stack(shifted[0]) * cw_ref[0:1, :]
    xc = xc + jnp.stack(shifted[1]) * cw_ref[1:2, :]
    xc = xc + jnp.stack(shifted[2]) * cw_ref[2:3, :]
    xc = xc + x * cw_ref[3:4, :]
    xc = xc + cb_ref[...]
    xbuf[:, 0:SUBLANES, :] = x[:, tt - SUBLANES:, :]

    for n in range(nslab):
        cols = slice(n * LRU_BS, (n + 1) * LRU_BS)
        xn = xc[:, :, cols].reshape(SUBLANES * tt, LRU_BS)
        g = jnp.dot(xn.astype(BF16), wax_ref[n], preferred_element_type=F32)
        r = jax.nn.sigmoid(g[:, :LRU_BS] + ba_ref[:, cols])
        i = jax.nn.sigmoid(g[:, LRU_BS:] + bx_ref[:, cols])
        neg_lam = -lam_ref[:, cols]
        softplus = jnp.maximum(neg_lam, 0.0) + jnp.log1p(jnp.exp(-jnp.abs(neg_lam)))
        a = jnp.exp2((-LRU_C * LOG2E * softplus) * r)
        v = 1.0 - a * a
        bv = jnp.where(v > 0.0, v * lax.rsqrt(v), 0.0) * (i * xn)
        for b in range(SUBLANES):
            a_scr[n, b * pitch:b * pitch + tt, :] = a[b * tt:(b + 1) * tt]
            b_scr[n, b * pitch:b * pitch + tt, :] = bv[b * tt:(b + 1) * tt]

    def step(tl, hs):
        rows = pl.ds(tl, SUBLANES, stride=pitch)
        hs = tuple(a_scr[n, rows, :] * hs[n] + b_scr[n, rows, :] for n in range(nslab))
        for n in range(nslab):
            h_scr[n, rows, :] = hs[n]
        return hs

    hs = tuple(carry_scr[:, n * LRU_BS:(n + 1) * LRU_BS] for n in range(nslab))
    hs = lax.fori_loop(0, tt, step, hs, unroll=8)
    for n in range(nslab):
        cols = slice(n * LRU_BS, (n + 1) * LRU_BS)
        carry_scr[:, cols] = hs[n]
        hlast_ref[:, cols] = hs[n]
        for b in range(SUBLANES):
            gate = _gelu_tanh(gr_ref[b, :, cols].astype(F32))
            y_ref[b, :, cols] = (h_scr[n, b * pitch:b * pitch + tt, :] * gate).astype(BF16)


def _rnn(z3, ctx, h0, conv_w, conv_b, w_ax, b_a, b_x, lam, *, tt, cb):
    b, t_len, _ = z3.shape
    assert b == SUBLANES
    ncb = D_MODEL // cb
    nblk = cb // LRU_BS
    pitch = tt + SUBLANES if (tt // SUBLANES) % 2 == 0 else tt + 2 * SUBLANES
    vec = lambda rows: pl.BlockSpec((rows, cb), lambda ci, ti: (0, ci))
    blocks = 3 * b * tt * cb * 2 + nblk * LRU_BS * 2 * LRU_BS * 2 + b * cb * 4
    resident = b * (tt + SUBLANES) * cb * 4 + 3 * b * pitch * cb * 4 + 6 * b * tt * cb * 4
    return pl.pallas_call(
        functools.partial(_rnn_kernel, tt=tt, cb=cb, pitch=pitch),
        grid=(ncb, t_len // tt),
        in_specs=[
            pl.BlockSpec((b, tt, cb), lambda ci, ti: (0, ti, P_XR * ncb + ci)),
            pl.BlockSpec((b, tt, cb), lambda ci, ti: (0, ti, P_GR * ncb + ci)),
            vec(SUBLANES), vec(1), vec(CONV_W), vec(1),
            pl.BlockSpec((nblk, LRU_BS, 2 * LRU_BS), lambda ci, ti: (ci, 0, 0)),
            vec(1), vec(1), vec(1),
        ],
        out_specs=[
            pl.BlockSpec((b, tt, cb), lambda ci, ti: (0, ti, ci)),
            pl.BlockSpec((b, cb), lambda ci, ti: (0, ci)),
        ],
        out_shape=[
            jax.ShapeDtypeStruct((b, t_len, D_MODEL), BF16),
            jax.ShapeDtypeStruct((b, D_MODEL), F32),
        ],
        scratch_shapes=[
            pltpu.VMEM((b, 2 * SUBLANES, cb), F32),
            pltpu.VMEM((nblk, b * pitch, LRU_BS), F32),
            pltpu.VMEM((nblk, b * pitch, LRU_BS), F32),
            pltpu.VMEM((nblk, b * pitch, LRU_BS), F32),
            pltpu.VMEM((b, cb), F32),
        ],
        compiler_params=pltpu.CompilerParams(
            dimension_semantics=("parallel", "arbitrary"),
            vmem_limit_bytes=_vmem_limit(blocks, resident)),
        name="rnn",
    )(z3, z3, ctx, h0, conv_w, conv_b, w_ax, b_a, b_x, lam)


def _forget_kernel(x_ref, fb_ref, nf_ref, *, n_blocks, first_valid):
    lane = lax.broadcasted_iota(jnp.int32, (N_HEADS, LANES), 1)
    fb = fb_ref[...]
    run = jnp.zeros((N_HEADS, 1), F32)
    for k in range(n_blocks):
        x = x_ref[0, :, k * LANES:(k + 1) * LANES] + fb
        lf = jnp.minimum(x, 0.0) - jnp.log1p(jnp.exp(-jnp.abs(x)))
        if k == 0:
            lf = jnp.where(lane >= first_valid, lf, 0.0)
        d = 1
        while d < LANES:
            lf = jnp.where(lane >= d, lf + pltpu.roll(lf, d, axis=1), lf)
            d *= 2
        f = lf + run
        run = f[:, LANES - 1:LANES]
        neg = f * -LOG2E
        if k == 0:
            neg = jnp.where(lane >= first_valid, neg, NEG_BIG)
        nf_ref[0, :, k * LANES:(k + 1) * LANES] = neg


def _forget_prefix(x_all, forget_b):
    b, _, n = x_all.shape
    return pl.pallas_call(
        functools.partial(_forget_kernel, n_blocks=n // LANES, first_valid=LANES - N_META),
        grid=(b,),
        in_specs=[
            pl.BlockSpec((1, N_HEADS, n), lambda bi: (bi, 0, 0)),
            pl.BlockSpec((N_HEADS, 1), lambda bi: (0, 0)),
        ],
        out_specs=pl.BlockSpec((1, N_HEADS, n), lambda bi: (bi, 0, 0)),
        out_shape=jax.ShapeDtypeStruct((b, N_HEADS, n), F32),
        compiler_params=pltpu.CompilerParams(dimension_semantics=("parallel",)),
        name="forget_prefix",
    )(x_all, forget_b)


def _attn_kernel(q_ref, k_ref, v_ref, km_ref, vm_ref, nf_ref, o_ref, vx_scr, vmx_scr, *, tq, nq, hg):
    qi = pl.program_id(2)
    causal = (lax.broadcasted_iota(jnp.int32, (tq, tq), 0)
              >= lax.broadcasted_iota(jnp.int32, (tq, tq), 1))

    @pl.when(qi == 0)
    def _():
        for hh in range(hg):
            cols = slice(hh * HEAD_DIM, (hh + 1) * HEAD_DIM)
            vx_scr[hh, :, 0:HEAD_DIM] = v_ref[0, :, cols]
            vx_scr[hh, :, HEAD_DIM:] = jnp.ones((vx_scr.shape[1], HEAD_DIM), BF16)
            vmx_scr[hh, :, 0:HEAD_DIM] = vm_ref[:, cols]
            vmx_scr[hh, :, HEAD_DIM:] = jnp.ones((LANES, HEAD_DIM), BF16)

    def head(hh, c):
        cols = slice(hh * HEAD_DIM, (hh + 1) * HEAD_DIM)
        q = q_ref[0, :, cols]

        def scores(k_blk, neg_f):
            s = lax.dot_general(q, k_blk, (((1,), (1,)), ((), ())), preferred_element_type=F32)
            return s + neg_f

        lo = c * tq
        ss = [scores(km_ref[:, cols], nf_ref[hh, :, 0:LANES])]
        vs = [vmx_scr[hh]]
        if c > 0:
            ss.append(scores(k_ref[0, 0:lo, cols], nf_ref[hh, :, LANES:LANES + lo]))
            vs.append(vx_scr[hh, 0:lo, :])
        s_diag = scores(k_ref[0, lo:lo + tq, cols], nf_ref[hh, :, LANES + lo:LANES + lo + tq])
        ss.append(jnp.where(causal, s_diag, NEG_BIG))
        vs.append(vx_scr[hh, lo:lo + tq, :])
        m = functools.reduce(jnp.maximum, [jnp.max(s, axis=1, keepdims=True) for s in ss])
        ol = jnp.zeros((tq, 2 * HEAD_DIM), F32)
        for s, v_blk in zip(ss, vs):
            ol = ol + jnp.dot(jnp.exp2(s - m).astype(BF16), v_blk, preferred_element_type=F32)
        o_ref[0, :, cols] = (ol[:, :HEAD_DIM] / ol[:, HEAD_DIM:]).astype(BF16)

    for c in range(nq):
        @pl.when(qi == c)
        def _(c=c):
            for hh in range(hg):
                head(hh, c)


def _attention(z3, zm_pad, neg_f, *, tq, hg):
    b, t_len, _ = z3.shape
    nq = t_len // tq
    ng = N_HEADS // hg
    w = hg * HEAD_DIM
    blocks = (tq * w * 2 * 2 + 2 * t_len * w * 2 + 2 * LANES * w * 2
              + hg * SUBLANES * (LANES + t_len) * 4)
    resident = hg * 4 * tq * (t_len + LANES) * 4 + hg * (t_len + LANES) * 2 * HEAD_DIM * 2
    return pl.pallas_call(
        functools.partial(_attn_kernel, tq=tq, nq=nq, hg=hg),
        grid=(b, ng, nq),
        in_specs=[
            pl.BlockSpec((1, tq, w), lambda bi, gi, qi: (bi, qi, P_Q * ng + gi)),
            pl.BlockSpec((1, t_len, w), lambda bi, gi, qi: (bi, 0, P_K * ng + gi)),
            pl.BlockSpec((1, t_len, w), lambda bi, gi, qi: (bi, 0, P_V * ng + gi)),
            pl.BlockSpec((LANES, w), lambda bi, gi, qi: (0, P_K * ng + gi)),
            pl.BlockSpec((LANES, w), lambda bi, gi, qi: (0, P_V * ng + gi)),
            pl.BlockSpec((hg, 1, LANES + t_len), lambda bi, gi, qi: (bi * ng + gi, 0, 0)),
        ],
        out_specs=pl.BlockSpec((1, tq, w), lambda bi, gi, qi: (bi, qi, gi)),
        out_shape=jax.ShapeDtypeStruct((b, t_len, N_HEADS * HEAD_DIM), BF16),
        scratch_shapes=[
            pltpu.VMEM((hg, t_len, 2 * HEAD_DIM), BF16),
            pltpu.VMEM((hg, LANES, 2 * HEAD_DIM), BF16),
        ],
        compiler_params=pltpu.CompilerParams(
            dimension_semantics=("parallel", "parallel", "arbitrary"),
            vmem_limit_bytes=_vmem_limit(blocks, resident)),
        name="attention",
    )(z3, z3, z3, zm_pad, zm_pad, neg_f)


def _merge_kernel(yr_ref, ya_ref, wr_ref, wa_ref, gr_ref, ga_ref, o_ref):
    pr = jnp.dot(yr_ref[...], wr_ref[...], preferred_element_type=F32)
    pa = jnp.dot(ya_ref[...], wa_ref[...], preferred_element_type=F32)
    merged = (jax.nn.sigmoid(gr_ref[...].astype(F32)) * pr
              + jax.nn.sigmoid(ga_ref[...].astype(F32)) * pa)
    o_ref[...] = merged.astype(BF16)


def _merge(y_rnn, y_attn, w_rnn_out, w_attn_out, zg, *, tm, tn):
    m = y_rnn.shape[0]
    nn = D_MODEL // tn
    blocks = 2 * tm * D_MODEL * 2 + 2 * D_MODEL * tn * 2 + 3 * tm * tn * 2
    resident = 3 * tm * tn * 4
    return pl.pallas_call(
        _merge_kernel,
        grid=(m // tm, nn),
        in_specs=[
            pl.BlockSpec((tm, D_MODEL), lambda i, j: (i, 0)),
            pl.BlockSpec((tm, D_MODEL), lambda i, j: (i, 0)),
            pl.BlockSpec((D_MODEL, tn), lambda i, j: (0, j)),
            pl.BlockSpec((D_MODEL, tn), lambda i, j: (0, j)),
            pl.BlockSpec((tm, tn), lambda i, j: (i, P_GRNN * nn + j)),
            pl.BlockSpec((tm, tn), lambda i, j: (i, P_GATTN * nn + j)),
        ],
        out_specs=pl.BlockSpec((tm, tn), lambda i, j: (i, j)),
        out_shape=jax.ShapeDtypeStruct((m, D_MODEL), BF16),
        compiler_params=pltpu.CompilerParams(
            dimension_semantics=("parallel", "arbitrary"),
            vmem_limit_bytes=_vmem_limit(blocks, resident)),
        name="merge",
    )(y_rnn, y_attn, w_rnn_out, w_attn_out, zg, zg)


def _out_proj_kernel(x_ref, w_ref, h_ref, g_ref, o_ref):
    mix = jnp.dot(x_ref[...], w_ref[...], preferred_element_type=F32)
    o_ref[...] = h_ref[...] + _rms(mix, g_ref[...])


def _out_proj(merged, w_o, h, g_post, *, tm):
    m = merged.shape[0]
    blocks = tm * D_MODEL * (2 + 4 + 4) + D_MODEL * D_MODEL * 2
    resident = tm * D_MODEL * 4
    return pl.pallas_call(
        _out_proj_kernel,
        grid=(m // tm,),
        in_specs=[
            pl.BlockSpec((tm, D_MODEL), lambda i: (i, 0)),
            pl.BlockSpec((D_MODEL, D_MODEL), lambda i: (0, 0)),
            pl.BlockSpec((tm, D_MODEL), lambda i: (i, 0)),
            pl.BlockSpec((1, D_MODEL), lambda i: (0, 0)),
        ],
        out_specs=pl.BlockSpec((tm, D_MODEL), lambda i: (i, 0)),
        out_shape=jax.ShapeDtypeStruct((m, D_MODEL), F32),
        compiler_params=pltpu.CompilerParams(
            dimension_semantics=("parallel",),
            vmem_limit_bytes=_vmem_limit(blocks, resident)),
        name="out_proj",
    )(merged, w_o, h, g_post)


def kernel(x, meta_tokens, norm_g, ffn1_w_gu, ffn1_w_down, w_in, conv_w, conv_b, lru_w_a, lru_b_a,
           lru_w_x, lru_b_x, lru_lambda, forget_b, w_rnn_out, w_attn_out, w_o, ffn2_w_gu,
           ffn2_w_down):
    b, t_len, d = x.shape
    m = b * t_len
    g = [norm_g[0, k:k + 1] for k in range(6)]
    row = lambda v: v[0].reshape(1, -1)

    w_in_t = w_in[0].T.astype(BF16)
    w_fl_t = w_in_t[Z_COLS:Z_COLS + LANES]
    w_hi_t = w_in_t[Z_COLS + N_HEADS:]
    w_ax = jnp.concatenate([lru_w_a[0], lru_w_x[0]], axis=-1).astype(BF16)
    w_gu1, w_d1 = ffn1_w_gu[0].astype(BF16), ffn1_w_down[0].astype(BF16)
    w_gu2, w_d2 = ffn2_w_gu[0].astype(BF16), ffn2_w_down[0].astype(BF16)
    w_r, w_a, w_out = w_rnn_out[0].astype(BF16), w_attn_out[0].astype(BF16), w_o[0].astype(BF16)
    rnn_params = (conv_w[0], row(conv_b), w_ax, row(lru_b_a), row(lru_b_x), row(lru_lambda))

    _, u_m = _ffn(meta_tokens, g[0], w_gu1, w_d1, g[1], g[2], tm=N_META)
    z_m = _in_proj(u_m, w_in_t, tm=N_META, tn=PROJ_TN)
    _, fl_m = _gate_proj(u_m, w_hi_t[:LANES], w_fl_t, tm=N_META, tn=LANES)
    zeros = jnp.zeros((SUBLANES, D_MODEL), F32)
    _, h_meta = _rnn(jnp.broadcast_to(z_m[None], (SUBLANES, N_META, Z_COLS)), zeros, zeros[:1],
                     *rnn_params, tt=N_META, cb=RNN_CB)
    ctx = z_m[N_META - SUBLANES:, :D_MODEL].astype(F32)
    zm_pad = jnp.pad(z_m, ((LANES - N_META, 0), (0, 0)))

    h1, u2 = _ffn(x.reshape(m, d), g[0], w_gu1, w_d1, g[1], g[2], tm=FFN_TM)
    z = _in_proj(u2, w_in_t, tm=PROJ_TM, tn=PROJ_TN)
    zg, fl = _gate_proj(u2, w_hi_t, w_fl_t, tm=PROJ_TM, tn=PROJ_TN)
    z3 = z.reshape(b, t_len, Z_COLS)
    y_rnn, _ = _rnn(z3, ctx, h_meta[:1], *rnn_params, tt=RNN_TT, cb=RNN_CB)

    fl_meta = jnp.pad(fl_m[:, :N_HEADS].T, ((0, 0), (LANES - N_META, 0)))
    fl_real = fl[:, :N_HEADS].reshape(b, t_len, N_HEADS).transpose(0, 2, 1)
    x_all = jnp.concatenate([jnp.broadcast_to(fl_meta[None], (b, N_HEADS, LANES)), fl_real], axis=2)
    neg_f = _forget_prefix(x_all, forget_b[0].reshape(N_HEADS, 1))
    y_attn = _attention(z3, zm_pad, neg_f.reshape(b * N_HEADS, 1, LANES + t_len),
                        tq=ATTN_TQ, hg=ATTN_HEADS_PER_STEP)

    merged = _merge(y_rnn.reshape(m, d), y_attn.reshape(m, d), w_r, w_a, zg, tm=MERGE_TM, tn=MERGE_TN)
    h2 = _out_proj(merged, w_out, h1, g[3], tm=OUT_TM)
    (h3,) = _ffn(h2, g[4], w_gu2, w_d2, g[5], None, tm=FFN_TM)
    return h3.reshape(b, t_len, d)
```

```python
import functools

import jax
import jax.numpy as jnp
from jax import lax
from jax.experimental import pallas as pl
from jax.experimental.pallas import tpu as pltpu

D_MODEL = 2048
N_META = 16
D_FF = 5632
N_HEADS = 16
HEAD_DIM = 128
LRU_BS = 128
LRU_BLOCKS = D_MODEL // LRU_BS
CONV_W = 4
LRU_C = 8.0
EPS = 1e-6
(P_XR, P_GR, P_Q, P_K, P_V) = range(5)
Z_COLS = 5 * D_MODEL
(P_GRNN, P_GATTN) = range(2)
ZG_COLS = 2 * D_MODEL
FFN_TM, FFN_TF = 512, 512
PROJ_TM, PROJ_TN = 1024, 2048
RNN_TT, RNN_CB = 256, 512
ATTN_TQ, ATTN_HEADS_PER_STEP = 512, 4
MERGE_TM, MERGE_TN = 1024, 512
OUT_TM = 512

LANES = 128
SUBLANES = 8
BF16_ROWS = 2 * SUBLANES
VMEM_BYTES_V7X = 64 * 1024 * 1024
NEG_BIG = -1e30
LOG2E = 1.4426950408889634
QK_SCALE_LOG2 = HEAD_DIM ** -0.5 * LOG2E

F32 = jnp.float32
BF16 = jnp.bfloat16


def _vmem_limit(pipelined_bytes, resident_bytes):
    return min(2 * pipelined_bytes + resident_bytes + (4 << 20), VMEM_BYTES_V7X - (4 << 20))


def _rms(x, g):
    return (x * lax.rsqrt(jnp.mean(x * x, axis=-1, keepdims=True) + EPS)) * g


def _ffn_kernel(h_ref, gpre_ref, wg_ref, wu_ref, wd_ref, gpost_ref, *rest, with_next):
    if with_next:
        gnext_ref, out_ref, unext_ref, u_scr = rest
    else:
        out_ref, u_scr = rest
    j = pl.program_id(1)

    @pl.when(j == 0)
    def _():
        u_scr[...] = _rms(h_ref[...], gpre_ref[...]).astype(BF16)
        out_ref[...] = jnp.zeros_like(out_ref)

    u = u_scr[...]
    gate = jnp.dot(u, wg_ref[...], preferred_element_type=F32)
    up = jnp.dot(u, wu_ref[...], preferred_element_type=F32)
    act = ((gate * jax.nn.sigmoid(gate)) * up).astype(BF16)
    out_ref[...] += jnp.dot(act, wd_ref[...], preferred_element_type=F32)

    @pl.when(j == pl.num_programs(1) - 1)
    def _():
        h_new = h_ref[...] + _rms(out_ref[...], 0.5 * gpost_ref[...])
        out_ref[...] = h_new
        if with_next:
            unext_ref[...] = _rms(h_new, gnext_ref[...]).astype(BF16)


def _ffn(h, g_pre, w_gu, w_down, g_post, g_next, *, tm):
    m = h.shape[0]
    tf = FFN_TF
    nf = D_FF // tf
    with_next = g_next is not None
    row_tile = pl.BlockSpec((tm, D_MODEL), lambda i, j: (i, 0))
    gain = pl.BlockSpec((1, D_MODEL), lambda i, j: (0, 0))
    blocks = tm * D_MODEL * (4 + 4 + 2 * with_next) + 3 * D_MODEL * tf * 2
    resident = tm * D_MODEL * (2 + 4) + tm * tf * (4 + 4 + 2)
    return pl.pallas_call(
        functools.partial(_ffn_kernel, with_next=with_next),
        grid=(m // tm, nf),
        in_specs=[
            row_tile,
            gain,
            pl.BlockSpec((D_MODEL, tf), lambda i, j: (0, j)),
            pl.BlockSpec((D_MODEL, tf), lambda i, j: (0, j + nf)),
            pl.BlockSpec((tf, D_MODEL), lambda i, j: (j, 0)),
            gain,
        ] + [gain] * with_next,
        out_specs=[row_tile] * (1 + with_next),
        out_shape=[jax.ShapeDtypeStruct((m, D_MODEL), F32)]
        + [jax.ShapeDtypeStruct((m, D_MODEL), BF16)] * with_next,
        scratch_shapes=[pltpu.VMEM((tm, D_MODEL), BF16)],
        compiler_params=pltpu.CompilerParams(
            dimension_semantics=("parallel", "arbitrary"),
            vmem_limit_bytes=_vmem_limit(blocks, resident)),
        name="ffn",
    )(h, g_pre, w_gu, w_gu, w_down, g_post, *([g_next] * with_next))


def _dot_nt(x, wt):
    return lax.dot_general(x, wt, (((1,), (1,)), ((), ())), preferred_element_type=F32)


def _in_proj_kernel(u_ref, wt_ref, z_ref, *, tn):
    j = pl.program_id(0)
    is_q = (j >= P_Q * D_MODEL // tn) & (j < (P_Q + 1) * D_MODEL // tn)
    factor = jnp.where(is_q, QK_SCALE_LOG2, 1.0)
    z_ref[...] = (_dot_nt(u_ref[...], wt_ref[...]) * factor).astype(BF16)


def _in_proj(u, w_in_t, *, tm, tn):
    m = u.shape[0]
    blocks = tm * D_MODEL * 2 + D_MODEL * tn * 2 + tm * tn * 2
    resident = tm * tn * 4
    return pl.pallas_call(
        functools.partial(_in_proj_kernel, tn=tn),
        grid=(Z_COLS // tn, m // tm),
        in_specs=[
            pl.BlockSpec((tm, D_MODEL), lambda j, i: (i, 0)),
            pl.BlockSpec((tn, D_MODEL), lambda j, i: (j, 0)),
        ],
        out_specs=pl.BlockSpec((tm, tn), lambda j, i: (i, j)),
        out_shape=jax.ShapeDtypeStruct((m, Z_COLS), BF16),
        compiler_params=pltpu.CompilerParams(
            dimension_semantics=("parallel", "parallel"),
            vmem_limit_bytes=_vmem_limit(blocks, resident)),
        name="in_proj",
    )(u, w_in_t)


def _gate_proj_kernel(u_ref, wt_ref, wflt_ref, zg_ref, fl_ref):
    u = u_ref[...]
    zg_ref[...] = _dot_nt(u, wt_ref[...]).astype(BF16)

    @pl.when(pl.program_id(1) == 0)
    def _():
        fl_ref[...] = _dot_nt(u, wflt_ref[...])


def _gate_proj(u, w_in_t, n_cols, *, tm, tn):
    m = u.shape[0]
    blocks = tm * D_MODEL * 2 + D_MODEL * tn * 2 + D_MODEL * LANES * 2 + tm * tn * 2 + tm * LANES * 4
    resident = tm * tn * 4
    return pl.pallas_call(
        _gate_proj_kernel,
        grid=(m // tm, n_cols // tn),
        in_specs=[
            pl.BlockSpec((tm, D_MODEL), lambda i, j: (i, 0)),
            pl.BlockSpec((pl.Element(tn), pl.Element(D_MODEL)),
                         lambda i, j: (pl.multiple_of(Z_COLS + N_HEADS + j * tn, BF16_ROWS), 0)),
            pl.BlockSpec((pl.Element(LANES), pl.Element(D_MODEL)), lambda i, j: (Z_COLS, 0)),
        ],
        out_specs=[
            pl.BlockSpec((tm, tn), lambda i, j: (i, j)),
            pl.BlockSpec((tm, LANES), lambda i, j: (i, 0)),
        ],
        out_shape=[
            jax.ShapeDtypeStruct((m, n_cols), BF16),
            jax.ShapeDtypeStruct((m, LANES), F32),
        ],
        compiler_params=pltpu.CompilerParams(
            dimension_semantics=("parallel", "arbitrary"),
            vmem_limit_bytes=_vmem_limit(blocks, resident)),
        name="gate_proj",
    )(u, w_in_t, w_in_t)


def _gelu_tanh(x):
    c = (2.0 / jnp.pi) ** 0.5
    half = 0.5 * x
    return half + half * jnp.tanh(x * (c + (c * 0.044715) * (x * x)))


def _rnn_kernel(xr_ref, gr_ref, ctx_ref, h0_ref, cw_ref, cb_ref, wax_ref, ba_ref, bx_ref, lam_ref,
                y_ref, hlast_ref, xbuf, a_scr, b_scr, h_scr, carry_scr, *, tt, cb, pitch):
    t = pl.program_id(1)
    nslab = cb // LRU_BS

    @pl.when(t == 0)
    def _():
        xbuf[:, 0:SUBLANES, :] = jnp.broadcast_to(ctx_ref[...][None], (SUBLANES, SUBLANES, cb))
        carry_scr[...] = jnp.broadcast_to(h0_ref[...], (SUBLANES, cb))

    xb = xr_ref[...]
    x = xb.astype(F32)
    xbuf[:, SUBLANES:, :] = x[:, 0:SUBLANES, :]
    shift = (lax.broadcasted_iota(jnp.int32, (tt, tt), 0)
             - lax.broadcasted_iota(jnp.int32, (tt, tt), 1))
    select = jnp.concatenate([(shift == k).astype(BF16) for k in (3, 2, 1)], axis=0)
    heads = [xbuf[:, pl.ds(SUBLANES - k, SUBLANES), :] for k in (3, 2, 1)]
    shifted = [[], [], []]
    for b in range(SUBLANES):
        body = jnp.dot(select, xb[b], preferred_element_type=F32)
        for s in range(3):
            shifted[s].append(jnp.concatenate(
                [heads[s][b], body[s * tt + SUBLANES:(s + 1) * tt]], axis=0))
    xc = jnp.stack(shifted[0]) * cw_ref[0:1, :]
    xc = xc + jnp.stack(shifted[1]) * cw_ref[1:2, :]
    xc = xc + jnp.stack(shifted[2]) * cw_ref[2:3, :]
    xc = xc + x * cw_ref[3:4, :]
    xc = xc + cb_ref[...]
    xbuf[:, 0:SUBLANES, :] = x[:, tt - SUBLANES:, :]

    for n in range(nslab):
        cols = slice(n * LRU_BS, (n + 1) * LRU_BS)
        xn = xc[:, :, cols].reshape(SUBLANES * tt, LRU_BS)
        g = jnp.dot(xn.astype(BF16), wax_ref[n], preferred_element_type=F32)
        r = jax.nn.sigmoid(g[:, :LRU_BS] + ba_ref[:, cols])
        i = jax.nn.sigmoid(g[:, LRU_BS:] + bx_ref[:, cols])
        neg_lam = -lam_ref[:, cols]
        softplus = jnp.maximum(neg_lam, 0.0) + jnp.log1p(jnp.exp(-jnp.abs(neg_lam)))
        a = jnp.exp2((-LRU_C * LOG2E * softplus) * r)
        v = 1.0 - a * a
        bv = jnp.where(v > 0.0, v * lax.rsqrt(v), 0.0) * (i * xn)
        for b in range(SUBLANES):
            a_scr[n, b * pitch:b * pitch + tt, :] = a[b * tt:(b + 1) * tt]
            b_scr[n, b * pitch:b * pitch + tt, :] = bv[b * tt:(b + 1) * tt]

    def step(tl, hs):
        rows = pl.ds(tl, SUBLANES, stride=pitch)
        hs = tuple(a_scr[n, rows, :] * hs[n] + b_scr[n, rows, :] for n in range(nslab))
        for n in range(nslab):
            h_scr[n, rows, :] = hs[n]
        return hs

    hs = tuple(carry_scr[:, n * LRU_BS:(n + 1) * LRU_BS] for n in range(nslab))
    hs = lax.fori_loop(0, tt, step, hs, unroll=8)
    for n in range(nslab):
        cols = slice(n * LRU_BS, (n + 1) * LRU_BS)
        carry_scr[:, cols] = hs[n]
        hlast_ref[:, cols] = hs[n]
        for b in range(SUBLANES):
            gate = _gelu_tanh(gr_ref[b, :, cols].astype(F32))
            y_ref[b, :, cols] = (h_scr[n, b * pitch:b * pitch + tt, :] * gate).astype(BF16)


def _rnn(z3, ctx, h0, conv_w, conv_b, w_ax, b_a, b_x, lam, *, tt, cb):
    b, t_len, _ = z3.shape
    assert b == SUBLANES
    ncb = D_MODEL // cb
    nblk = cb // LRU_BS
    pitch = tt + SUBLANES if (tt // SUBLANES) % 2 == 0 else tt + 2 * SUBLANES
    vec = lambda rows: pl.BlockSpec((rows, cb), lambda ci, ti: (0, ci))
    blocks = 3 * b * tt * cb * 2 + nblk * LRU_BS * 2 * LRU_BS * 2 + b * cb * 4
    resident = b * (tt + SUBLANES) * cb * 4 + 3 * b * pitch * cb * 4 + 6 * b * tt * cb * 4
    return pl.pallas_call(
        functools.partial(_rnn_kernel, tt=tt, cb=cb, pitch=pitch),
        grid=(ncb, t_len // tt),
        in_specs=[
            pl.BlockSpec((b, tt, cb), lambda ci, ti: (0, ti, P_XR * ncb + ci)),
            pl.BlockSpec((b, tt, cb), lambda ci, ti: (0, ti, P_GR * ncb + ci)),
            vec(SUBLANES), vec(1), vec(CONV_W), vec(1),
            pl.BlockSpec((nblk, LRU_BS, 2 * LRU_BS), lambda ci, ti: (ci, 0, 0)),
            vec(1), vec(1), vec(1),
        ],
        out_specs=[
            pl.BlockSpec((b, tt, cb), lambda ci, ti: (0, ti, ci)),
            pl.BlockSpec((b, cb), lambda ci, ti: (0, ci)),
        ],
        out_shape=[
            jax.ShapeDtypeStruct((b, t_len, D_MODEL), BF16),
            jax.ShapeDtypeStruct((b, D_MODEL), F32),
        ],
        scratch_shapes=[
            pltpu.VMEM((b, 2 * SUBLANES, cb), F32),
            pltpu.VMEM((nblk, b * pitch, LRU_BS), F32),
            pltpu.VMEM((nblk, b * pitch, LRU_BS), F32),
            pltpu.VMEM((nblk, b * pitch, LRU_BS), F32),
            pltpu.VMEM((b, cb), F32),
        ],
        compiler_params=pltpu.CompilerParams(
            dimension_semantics=("parallel", "arbitrary"),
            vmem_limit_bytes=_vmem_limit(blocks, resident)),
        name="rnn",
    )(z3, z3, ctx, h0, conv_w, conv_b, w_ax, b_a, b_x, lam)


def _forget_kernel(x_ref, fb_ref, nf_ref, *, n_blocks, first_valid):
    lane = lax.broadcasted_iota(jnp.int32, (N_HEADS, LANES), 1)
    fb = fb_ref[...]
    run = jnp.zeros((N_HEADS, 1), F32)
    for k in range(n_blocks):
        x = x_ref[0, :, k * LANES:(k + 1) * LANES] + fb
        lf = jnp.minimum(x, 0.0) - jnp.log1p(jnp.exp(-jnp.abs(x)))
        if k == 0:
            lf = jnp.where(lane >= first_valid, lf, 0.0)
        d = 1
        while d < LANES:
            lf = jnp.where(lane >= d, lf + pltpu.roll(lf, d, axis=1), lf)
            d *= 2
        f = lf + run
        run = f[:, LANES - 1:LANES]
        neg = f * -LOG2E
        if k == 0:
            neg = jnp.where(lane >= first_valid, neg, NEG_BIG)
        nf_ref[0, :, k * LANES:(k + 1) * LANES] = neg


def _forget_prefix(x_all, forget_b):
    b, _, n = x_all.shape
    return pl.pallas_call(
        functools.partial(_forget_kernel, n_blocks=n // LANES, first_valid=LANES - N_META),
        grid=(b,),
        in_specs=[
            pl.BlockSpec((1, N_HEADS, n), lambda bi: (bi, 0, 0)),
            pl.BlockSpec((N_HEADS, 1), lambda bi: (0, 0)),
        ],
        out_specs=pl.BlockSpec((1, N_HEADS, n), lambda bi: (bi, 0, 0)),
        out_shape=jax.ShapeDtypeStruct((b, N_HEADS, n), F32),
        compiler_params=pltpu.CompilerParams(dimension_semantics=("parallel",)),
        name="forget_prefix",
    )(x_all, forget_b)


def _attn_kernel(q_ref, k_ref, v_ref, km_ref, vm_ref, nf_ref, o_ref, vx_scr, vmx_scr, *, tq, nq, hg):
    qi = pl.program_id(2)
    causal = (lax.broadcasted_iota(jnp.int32, (tq, tq), 0)
              >= lax.broadcasted_iota(jnp.int32, (tq, tq), 1))

    @pl.when(qi == 0)
    def _():
        for hh in range(hg):
            cols = slice(hh * HEAD_DIM, (hh + 1) * HEAD_DIM)
            vx_scr[hh, :, 0:HEAD_DIM] = v_ref[0, :, cols]
            vx_scr[hh, :, HEAD_DIM:] = jnp.ones((vx_scr.shape[1], HEAD_DIM), BF16)
            vmx_scr[hh, :, 0:HEAD_DIM] = vm_ref[:, cols]
            vmx_scr[hh, :, HEAD_DIM:] = jnp.ones((LANES, HEAD_DIM), BF16)

    def head(hh, c):
        cols = slice(hh * HEAD_DIM, (hh + 1) * HEAD_DIM)
        q = q_ref[0, :, cols]

        def scores(k_blk, neg_f):
            s = lax.dot_general(q, k_blk, (((1,), (1,)), ((), ())), preferred_element_type=F32)
            return s + neg_f

        lo = c * tq
        ss = [scores(km_ref[:, cols], nf_ref[hh, :, 0:LANES])]
        vs = [vmx_scr[hh]]
        if c > 0:
            ss.append(scores(k_ref[0, 0:lo, cols], nf_ref[hh, :, LANES:LANES + lo]))
            vs.append(vx_scr[hh, 0:lo, :])
        s_diag = scores(k_ref[0, lo:lo + tq, cols], nf_ref[hh, :, LANES + lo:LANES + lo + tq])
        ss.append(jnp.where(causal, s_diag, NEG_BIG))
        vs.append(vx_scr[hh, lo:lo + tq, :])
        m = functools.reduce(jnp.maximum, [jnp.max(s, axis=1, keepdims=True) for s in ss])
        ol = jnp.zeros((tq, 2 * HEAD_DIM), F32)
        for s, v_blk in zip(ss, vs):
            ol = ol + jnp.dot(jnp.exp2(s - m).astype(BF16), v_blk, preferred_element_type=F32)
        o_ref[0, :, cols] = (ol[:, :HEAD_DIM] / ol[:, HEAD_DIM:]).astype(BF16)

    for c in range(nq):
        @pl.when(qi == c)
        def _(c=c):
            for hh in range(hg):
                head(hh, c)


def _attention(z3, zm_pad, neg_f, *, tq, hg):
    b, t_len, _ = z3.shape
    nq = t_len // tq
    ng = N_HEADS // hg
    w = hg * HEAD_DIM
    blocks = (tq * w * 2 * 2 + 2 * t_len * w * 2 + 2 * LANES * w * 2
              + hg * SUBLANES * (LANES + t_len) * 4)
    resident = hg * 4 * tq * (t_len + LANES) * 4 + hg * (t_len + LANES) * 2 * HEAD_DIM * 2
    return pl.pallas_call(
        functools.partial(_attn_kernel, tq=tq, nq=nq, hg=hg),
        grid=(b, ng, nq),
        in_specs=[
            pl.BlockSpec((1, tq, w), lambda bi, gi, qi: (bi, qi, P_Q * ng + gi)),
            pl.BlockSpec((1, t_len, w), lambda bi, gi, qi: (bi, 0, P_K * ng + gi)),
            pl.BlockSpec((1, t_len, w), lambda bi, gi, qi: (bi, 0, P_V * ng + gi)),
            pl.BlockSpec((LANES, w), lambda bi, gi, qi: (0, P_K * ng + gi)),
            pl.BlockSpec((LANES, w), lambda bi, gi, qi: (0, P_V * ng + gi)),
            pl.BlockSpec((hg, 1, LANES + t_len), lambda bi, gi, qi: (bi * ng + gi, 0, 0)),
        ],
        out_specs=pl.BlockSpec((1, tq, w), lambda bi, gi, qi: (bi, qi, gi)),
        out_shape=jax.ShapeDtypeStruct((b, t_len, N_HEADS * HEAD_DIM), BF16),
        scratch_shapes=[
            pltpu.VMEM((hg, t_len, 2 * HEAD_DIM), BF16),
            pltpu.VMEM((hg, LANES, 2 * HEAD_DIM), BF16),
        ],
        compiler_params=pltpu.CompilerParams(
            dimension_semantics=("parallel", "parallel", "arbitrary"),
            vmem_limit_bytes=_vmem_limit(blocks, resident)),
        name="attention",
    )(z3, z3, z3, zm_pad, zm_pad, neg_f)


def _merge_kernel(yr_ref, ya_ref, wr_ref, wa_ref, gr_ref, ga_ref, o_ref):
    pr = jnp.dot(yr_ref[...], wr_ref[...], preferred_element_type=F32)
    pa = jnp.dot(ya_ref[...], wa_ref[...], preferred_element_type=F32)
    merged = (jax.nn.sigmoid(gr_ref[...].astype(F32)) * pr
              + jax.nn.sigmoid(ga_ref[...].astype(F32)) * pa)
    o_ref[...] = merged.astype(BF16)


def _merge(y_rnn, y_attn, w_rnn_out, w_attn_out, zg, *, tm, tn):
    m = y_rnn.shape[0]
    nn = D_MODEL // tn
    blocks = 2 * tm * D_MODEL * 2 + 2 * D_MODEL * tn * 2 + 3 * tm * tn * 2
    resident = 3 * tm * tn * 4
    return pl.pallas_call(
        _merge_kernel,
        grid=(m // tm, nn),
        in_specs=[
            pl.BlockSpec((tm, D_MODEL), lambda i, j: (i, 0)),
            pl.BlockSpec((tm, D_MODEL), lambda i, j: (i, 0)),
            pl.BlockSpec((D_MODEL, tn), lambda i, j: (0, j)),
            pl.BlockSpec((D_MODEL, tn), lambda i, j: (0, j)),
            pl.BlockSpec((tm, tn), lambda i, j: (i, P_GRNN * nn + j)),
            pl.BlockSpec((tm, tn), lambda i, j: (i, P_GATTN * nn + j)),
        ],
        out_specs=pl.BlockSpec((tm, tn), lambda i, j: (i, j)),
        out_shape=jax.ShapeDtypeStruct((m, D_MODEL), BF16),
        compiler_params=pltpu.CompilerParams(
            dimension_semantics=("parallel", "arbitrary"),
            vmem_limit_bytes=_vmem_limit(blocks, resident)),
        name="merge",
    )(y_rnn, y_attn, w_rnn_out, w_attn_out, zg, zg)


def _out_proj_kernel(x_ref, w_ref, h_ref, g_ref, o_ref):
    mix = jnp.dot(x_ref[...], w_ref[...], preferred_element_type=F32)
    o_ref[...] = h_ref[...] + _rms(mix, g_ref[...])


def _out_proj(merged, w_o, h, g_post, *, tm):
    m = merged.shape[0]
    blocks = tm * D_MODEL * (2 + 4 + 4) + D_MODEL * D_MODEL * 2
    resident = tm * D_MODEL * 4
    return pl.pallas_call(
        _out_proj_kernel,
        grid=(m // tm,),
        in_specs=[
            pl.BlockSpec((tm, D_MODEL), lambda i: (i, 0)),
            pl.BlockSpec((D_MODEL, D_MODEL), lambda i: (0, 0)),
            pl.BlockSpec((tm, D_MODEL), lambda i: (i, 0)),
            pl.BlockSpec((1, D_MODEL), lambda i: (0, 0)),
        ],
        out_specs=pl.BlockSpec((tm, D_MODEL), lambda i: (i, 0)),
        out_shape=jax.ShapeDtypeStruct((m, D_MODEL), F32),
        compiler_params=pltpu.CompilerParams(
            dimension_semantics=("parallel",),
            vmem_limit_bytes=_vmem_limit(blocks, resident)),
        name="out_proj",
    )(merged, w_o, h, g_post)


def kernel(x, meta_tokens, norm_g, ffn1_w_gu, ffn1_w_down, w_in, conv_w, conv_b, lru_w_a, lru_b_a,
           lru_w_x, lru_b_x, lru_lambda, forget_b, w_rnn_out, w_attn_out, w_o, ffn2_w_gu,
           ffn2_w_down):
    b, t_len, d = x.shape
    m = b * t_len
    g = [norm_g[0, k:k + 1] for k in range(6)]
    row = lambda v: v[0].reshape(1, -1)

    w_in_t = w_in[0].T.astype(BF16)
    w_ax = jnp.concatenate([lru_w_a[0], lru_w_x[0]], axis=-1).astype(BF16)
    w_gu1, w_d1 = ffn1_w_gu[0].astype(BF16), ffn1_w_down[0].astype(BF16)
    w_gu2, w_d2 = ffn2_w_gu[0].astype(BF16), ffn2_w_down[0].astype(BF16)
    w_r, w_a, w_out = w_rnn_out[0].astype(BF16), w_attn_out[0].astype(BF16), w_o[0].astype(BF16)
    rnn_params = (conv_w[0], row(conv_b), w_ax, row(lru_b_a), row(lru_b_x), row(lru_lambda))

    _, u_m = _ffn(meta_tokens, g[0], w_gu1, w_d1, g[1], g[2], tm=N_META)
    z_m = _in_proj(u_m, w_in_t, tm=N_META, tn=PROJ_TN)
    _, fl_m = _gate_proj(u_m, w_in_t, LANES, tm=N_META, tn=LANES)
    zeros = jnp.zeros((SUBLANES, D_MODEL), F32)
    _, h_meta = _rnn(jnp.broadcast_to(z_m[None], (SUBLANES, N_META, Z_COLS)), zeros, zeros[:1],
                     *rnn_params, tt=N_META, cb=RNN_CB)
    ctx = z_m[N_META - SUBLANES:, :D_MODEL].astype(F32)
    zm_pad = jnp.pad(z_m, ((LANES - N_META, 0), (0, 0)))

    h1, u2 = _ffn(x.reshape(m, d), g[0], w_gu1, w_d1, g[1], g[2], tm=FFN_TM)
    z = _in_proj(u2, w_in_t, tm=PROJ_TM, tn=PROJ_TN)
    zg, fl = _gate_proj(u2, w_in_t, ZG_COLS, tm=PROJ_TM, tn=PROJ_TN)
    z3 = z.reshape(b, t_len, Z_COLS)
    y_rnn, _ = _rnn(z3, ctx, h_meta[:1], *rnn_params, tt=RNN_TT, cb=RNN_CB)

    fl_meta = jnp.pad(fl_m[:, :N_HEADS].T, ((0, 0), (LANES - N_META, 0)))
    fl_real = fl[:, :N_HEADS].reshape(b, t_len, N_HEADS).transpose(0, 2, 1)
    x_all = jnp.concatenate([jnp.broadcast_to(fl_meta[None], (b, N_HEADS, LANES)), fl_real], axis=2)
    neg_f = _forget_prefix(x_all, forget_b[0].reshape(N_HEADS, 1))
    y_attn = _attention(z3, zm_pad, neg_f.reshape(b * N_HEADS, 1, LANES + t_len),
                        tq=ATTN_TQ, hg=ATTN_HEADS_PER_STEP)

    merged = _merge(y_rnn.reshape(m, d), y_attn.reshape(m, d), w_r, w_a, zg, tm=MERGE_TM, tn=MERGE_TN)
    h2 = _out_proj(merged, w_out, h1, g[3], tm=OUT_TM)
    (h3,) = _ffn(h2, g[4], w_gu2, w_d2, g[5], None, tm=FFN_TM)
    return h3.reshape(b, t_len, d)
```

```python
import functools

import jax
import jax.numpy as jnp
from jax import lax
from jax.experimental import pallas as pl
from jax.experimental.pallas import tpu as pltpu

D_MODEL = 2048
N_META = 16
D_FF = 5632
N_HEADS = 16
HEAD_DIM = 128
LRU_BS = 128
LRU_BLOCKS = D_MODEL // LRU_BS
CONV_W = 4
LRU_C = 8.0
EPS = 1e-6
(P_XR, P_GR, P_Q, P_K, P_V) = range(5)
Z_COLS = 5 * D_MODEL
(P_GRNN, P_GATTN) = range(2)
ZG_COLS = 2 * D_MODEL
FFN_TM, FFN_TF = 512, 512
PROJ_TM, PROJ_TN = 1024, 2048
RNN_TT, RNN_CB = 256, 512
ATTN_TQ, ATTN_HEADS_PER_STEP = 512, 4
MERGE_TM, MERGE_TN = 1024, 512
OUT_TM = 512

LANES = 128
SUBLANES = 8
BF16_ROWS = 2 * SUBLANES
VMEM_BYTES_V7X = 64 * 1024 * 1024
NEG_BIG = -1e30
LOG2E = 1.4426950408889634
QK_SCALE_LOG2 = HEAD_DIM ** -0.5 * LOG2E

F32 = jnp.float32
BF16 = jnp.bfloat16


def _vmem_limit(pipelined_bytes, resident_bytes):
    return min(2 * pipelined_bytes + resident_bytes + (4 << 20), VMEM_BYTES_V7X - (4 << 20))


def _rms(x, g):
    return (x * lax.rsqrt(jnp.mean(x * x, axis=-1, keepdims=True) + EPS)) * g


def _ffn_kernel(h_ref, gpre_ref, wg_ref, wu_ref, wd_ref, gpost_ref, *rest, with_next):
    if with_next:
        gnext_ref, out_ref, unext_ref, u_scr = rest
    else:
        out_ref, u_scr = rest
    j = pl.program_id(1)

    @pl.when(j == 0)
    def _():
        u_scr[...] = _rms(h_ref[...], gpre_ref[...]).astype(BF16)
        out_ref[...] = jnp.zeros_like(out_ref)

    u = u_scr[...]
    gate = jnp.dot(u, wg_ref[...], preferred_element_type=F32)
    up = jnp.dot(u, wu_ref[...], preferred_element_type=F32)
    act = ((gate * jax.nn.sigmoid(gate)) * up).astype(BF16)
    out_ref[...] += jnp.dot(act, wd_ref[...], preferred_element_type=F32)

    @pl.when(j == pl.num_programs(1) - 1)
    def _():
        h_new = h_ref[...] + _rms(out_ref[...], 0.5 * gpost_ref[...])
        out_ref[...] = h_new
        if with_next:
            unext_ref[...] = _rms(h_new, gnext_ref[...]).astype(BF16)


def _ffn(h, g_pre, w_gu, w_down, g_post, g_next, *, tm):
    m = h.shape[0]
    tf = FFN_TF
    nf = D_FF // tf
    with_next = g_next is not None
    row_tile = pl.BlockSpec((tm, D_MODEL), lambda i, j: (i, 0))
    gain = pl.BlockSpec((1, D_MODEL), lambda i, j: (0, 0))
    blocks = tm * D_MODEL * (4 + 4 + 2 * with_next) + 3 * D_MODEL * tf * 2
    resident = tm * D_MODEL * (2 + 4) + tm * tf * (4 + 4 + 2)
    return pl.pallas_call(
        functools.partial(_ffn_kernel, with_next=with_next),
        grid=(m // tm, nf),
        in_specs=[
            row_tile,
            gain,
            pl.BlockSpec((D_MODEL, tf), lambda i, j: (0, j)),
            pl.BlockSpec((D_MODEL, tf), lambda i, j: (0, j + nf)),
            pl.BlockSpec((tf, D_MODEL), lambda i, j: (j, 0)),
            gain,
        ] + [gain] * with_next,
        out_specs=[row_tile] * (1 + with_next),
        out_shape=[jax.ShapeDtypeStruct((m, D_MODEL), F32)]
        + [jax.ShapeDtypeStruct((m, D_MODEL), BF16)] * with_next,
        scratch_shapes=[pltpu.VMEM((tm, D_MODEL), BF16)],
        compiler_params=pltpu.CompilerParams(
            dimension_semantics=("parallel", "arbitrary"),
            vmem_limit_bytes=_vmem_limit(blocks, resident)),
        name="ffn",
    )(h, g_pre, w_gu, w_gu, w_down, g_post, *([g_next] * with_next))


def _dot_nt(x, wt):
    return lax.dot_general(x, wt, (((1,), (1,)), ((), ())), preferred_element_type=F32)


def _in_proj_kernel(u_ref, wt_ref, z_ref, *, tn):
    j = pl.program_id(0)
    is_q = (j >= P_Q * D_MODEL // tn) & (j < (P_Q + 1) * D_MODEL // tn)
    factor = jnp.where(is_q, QK_SCALE_LOG2, 1.0)
    z_ref[...] = (_dot_nt(u_ref[...], wt_ref[...]) * factor).astype(BF16)


def _in_proj(u, w_in_t, *, tm, tn):
    m = u.shape[0]
    blocks = tm * D_MODEL * 2 + D_MODEL * tn * 2 + tm * tn * 2
    resident = tm * tn * 4
    return pl.pallas_call(
        functools.partial(_in_proj_kernel, tn=tn),
        grid=(Z_COLS // tn, m // tm),
        in_specs=[
            pl.BlockSpec((tm, D_MODEL), lambda j, i: (i, 0)),
            pl.BlockSpec((tn, D_MODEL), lambda j, i: (j, 0)),
        ],
        out_specs=pl.BlockSpec((tm, tn), lambda j, i: (i, j)),
        out_shape=jax.ShapeDtypeStruct((m, Z_COLS), BF16),
        compiler_params=pltpu.CompilerParams(
            dimension_semantics=("parallel", "parallel"),
            vmem_limit_bytes=_vmem_limit(blocks, resident)),
        name="in_proj",
    )(u, w_in_t)


def _gate_proj_kernel(u_ref, wt_ref, wflt_ref, zg_ref, fl_ref):
    u = u_ref[...]
    zg_ref[...] = _dot_nt(u, wt_ref[...]).astype(BF16)

    @pl.when(pl.program_id(1) == 0)
    def _():
        fl_ref[...] = _dot_nt(u, wflt_ref[...])


def _gate_proj(u, w_in_t, n_cols, *, tm, tn):
    m = u.shape[0]
    blocks = tm * D_MODEL * 2 + D_MODEL * tn * 2 + D_MODEL * LANES * 2 + tm * tn * 2 + tm * LANES * 4
    resident = tm * tn * 4
    return pl.pallas_call(
        _gate_proj_kernel,
        grid=(m // tm, n_cols // tn),
        in_specs=[
            pl.BlockSpec((tm, D_MODEL), lambda i, j: (i, 0)),
            pl.BlockSpec((pl.Element(tn), pl.Element(D_MODEL)),
                         lambda i, j: (pl.multiple_of(Z_COLS + N_HEADS + j * tn, BF16_ROWS), 0)),
            pl.BlockSpec((pl.Element(LANES), pl.Element(D_MODEL)), lambda i, j: (Z_COLS, 0)),
        ],
        out_specs=[
            pl.BlockSpec((tm, tn), lambda i, j: (i, j)),
            pl.BlockSpec((tm, LANES), lambda i, j: (i, 0)),
        ],
        out_shape=[
            jax.ShapeDtypeStruct((m, n_cols), BF16),
            jax.ShapeDtypeStruct((m, LANES), F32),
        ],
        compiler_params=pltpu.CompilerParams(
            dimension_semantics=("parallel", "arbitrary"),
            vmem_limit_bytes=_vmem_limit(blocks, resident)),
        name="gate_proj",
    )(u, w_in_t, w_in_t)


def _gelu_tanh(x):
    c = (2.0 / jnp.pi) ** 0.5
    half = 0.5 * x
    return half + half * jnp.tanh(x * (c + (c * 0.044715) * (x * x)))


def _rnn_kernel(xr_ref, gr_ref, ctx_ref, h0_ref, cw_ref, cb_ref, wax_ref, ba_ref, bx_ref, lam_ref,
                y_ref, hlast_ref, xbuf, a_scr, b_scr, h_scr, carry_scr, *, tt, cb):
    t = pl.program_id(1)
    nslab = cb // LRU_BS

    @pl.when(t == 0)
    def _():
        xbuf[:, 0:SUBLANES, :] = jnp.broadcast_to(ctx_ref[...][None], (SUBLANES, SUBLANES, cb))
        carry_scr[...] = jnp.broadcast_to(h0_ref[...], (SUBLANES, cb))

    xb = xr_ref[...]
    x = xb.astype(F32)
    xbuf[:, SUBLANES:, :] = x[:, 0:SUBLANES, :]
    shift = (lax.broadcasted_iota(jnp.int32, (tt, tt), 0)
             - lax.broadcasted_iota(jnp.int32, (tt, tt), 1))
    select = jnp.concatenate([(shift == k).astype(BF16) for k in (3, 2, 1)], axis=0)
    heads = [xbuf[:, pl.ds(SUBLANES - k, SUBLANES), :] for k in (3, 2, 1)]
    shifted = [[], [], []]
    for b in range(SUBLANES):
        body = jnp.dot(select, xb[b], preferred_element_type=F32)
        for s in range(3):
            shifted[s].append(jnp.concatenate(
                [heads[s][b], body[s * tt + SUBLANES:(s + 1) * tt]], axis=0))
    xc = jnp.stack(shifted[0]) * cw_ref[0:1, :]
    xc = xc + jnp.stack(shifted[1]) * cw_ref[1:2, :]
    xc = xc + jnp.stack(shifted[2]) * cw_ref[2:3, :]
    xc = xc + x * cw_ref[3:4, :]
    xc = xc + cb_ref[...]
    xbuf[:, 0:SUBLANES, :] = x[:, tt - SUBLANES:, :]

    for n in range(nslab):
        cols = slice(n * LRU_BS, (n + 1) * LRU_BS)
        xn = xc[:, :, cols].reshape(SUBLANES * tt, LRU_BS)
        g = jnp.dot(xn.astype(BF16), wax_ref[n], preferred_element_type=F32)
        r = jax.nn.sigmoid(g[:, :LRU_BS] + ba_ref[:, cols])
        i = jax.nn.sigmoid(g[:, LRU_BS:] + bx_ref[:, cols])
        neg_lam = -lam_ref[:, cols]
        softplus = jnp.maximum(neg_lam, 0.0) + jnp.log1p(jnp.exp(-jnp.abs(neg_lam)))
        a = jnp.exp2((-LRU_C * LOG2E * softplus) * r)
        v = 1.0 - a * a
        bv = jnp.where(v > 0.0, v * lax.rsqrt(v), 0.0) * (i * xn)
        for b in range(SUBLANES):
            a_scr[n, pl.ds(b, tt, stride=SUBLANES), :] = a[b * tt:(b + 1) * tt]
            b_scr[n, pl.ds(b, tt, stride=SUBLANES), :] = bv[b * tt:(b + 1) * tt]

    def step(tl, hs):
        rows = pl.ds(pl.multiple_of(tl * SUBLANES, SUBLANES), SUBLANES)
        hs = tuple(a_scr[n, rows, :] * hs[n] + b_scr[n, rows, :] for n in range(nslab))
        for n in range(nslab):
            h_scr[n, rows, :] = hs[n]
        return hs

    hs = tuple(carry_scr[:, n * LRU_BS:(n + 1) * LRU_BS] for n in range(nslab))
    hs = lax.fori_loop(0, tt, step, hs, unroll=8)
    for n in range(nslab):
        cols = slice(n * LRU_BS, (n + 1) * LRU_BS)
        carry_scr[:, cols] = hs[n]
        hlast_ref[:, cols] = hs[n]
        for b in range(SUBLANES):
            gate = _gelu_tanh(gr_ref[b, :, cols].astype(F32))
            y_ref[b, :, cols] = (h_scr[n, pl.ds(b, tt, stride=SUBLANES), :] * gate).astype(BF16)


def _rnn(z3, ctx, h0, conv_w, conv_b, w_ax, b_a, b_x, lam, *, tt, cb):
    b, t_len, _ = z3.shape
    assert b == SUBLANES
    ncb = D_MODEL // cb
    nblk = cb // LRU_BS
    vec = lambda rows: pl.BlockSpec((rows, cb), lambda ci, ti: (0, ci))
    blocks = 3 * b * tt * cb * 2 + nblk * LRU_BS * 2 * LRU_BS * 2 + b * cb * 4
    resident = b * 2 * SUBLANES * cb * 4 + 3 * b * tt * cb * 4 + 6 * b * tt * cb * 4
    return pl.pallas_call(
        functools.partial(_rnn_kernel, tt=tt, cb=cb),
        grid=(ncb, t_len // tt),
        in_specs=[
            pl.BlockSpec((b, tt, cb), lambda ci, ti: (0, ti, P_XR * ncb + ci)),
            pl.BlockSpec((b, tt, cb), lambda ci, ti: (0, ti, P_GR * ncb + ci)),
            vec(SUBLANES), vec(1), vec(CONV_W), vec(1),
            pl.BlockSpec((nblk, LRU_BS, 2 * LRU_BS), lambda ci, ti: (ci, 0, 0)),
            vec(1), vec(1), vec(1),
        ],
        out_specs=[
            pl.BlockSpec((b, tt, cb), lambda ci, ti: (0, ti, ci)),
            pl.BlockSpec((b, cb), lambda ci, ti: (0, ci)),
        ],
        out_shape=[
            jax.ShapeDtypeStruct((b, t_len, D_MODEL), BF16),
            jax.ShapeDtypeStruct((b, D_MODEL), F32),
        ],
        scratch_shapes=[
            pltpu.VMEM((b, 2 * SUBLANES, cb), F32),
            pltpu.VMEM((nblk, b * tt, LRU_BS), F32),
            pltpu.VMEM((nblk, b * tt, LRU_BS), F32),
            pltpu.VMEM((nblk, b * tt, LRU_BS), F32),
            pltpu.VMEM((b, cb), F32),
        ],
        compiler_params=pltpu.CompilerParams(
            dimension_semantics=("parallel", "arbitrary"),
            vmem_limit_bytes=_vmem_limit(blocks, resident)),
        name="rnn",
    )(z3, z3, ctx, h0, conv_w, conv_b, w_ax, b_a, b_x, lam)


def _forget_kernel(x_ref, fb_ref, nf_ref, *, n_blocks, first_valid):
    lane = lax.broadcasted_iota(jnp.int32, (N_HEADS, LANES), 1)
    fb = fb_ref[...]
    run = jnp.zeros((N_HEADS, 1), F32)
    for k in range(n_blocks):
        x = x_ref[0, :, k * LANES:(k + 1) * LANES] + fb
        lf = jnp.minimum(x, 0.0) - jnp.log1p(jnp.exp(-jnp.abs(x)))
        if k == 0:
            lf = jnp.where(lane >= first_valid, lf, 0.0)
        d = 1
        while d < LANES:
            lf = jnp.where(lane >= d, lf + pltpu.roll(lf, d, axis=1), lf)
            d *= 2
        f = lf + run
        run = f[:, LANES - 1:LANES]
        neg = f * -LOG2E
        if k == 0:
            neg = jnp.where(lane >= first_valid, neg, NEG_BIG)
        nf_ref[0, :, k * LANES:(k + 1) * LANES] = neg


def _forget_prefix(x_all, forget_b):
    b, _, n = x_all.shape
    return pl.pallas_call(
        functools.partial(_forget_kernel, n_blocks=n // LANES, first_valid=LANES - N_META),
        grid=(b,),
        in_specs=[
            pl.BlockSpec((1, N_HEADS, n), lambda bi: (bi, 0, 0)),
            pl.BlockSpec((N_HEADS, 1), lambda bi: (0, 0)),
        ],
        out_specs=pl.BlockSpec((1, N_HEADS, n), lambda bi: (bi, 0, 0)),
        out_shape=jax.ShapeDtypeStruct((b, N_HEADS, n), F32),
        compiler_params=pltpu.CompilerParams(dimension_semantics=("parallel",)),
        name="forget_prefix",
    )(x_all, forget_b)


def _attn_kernel(q_ref, k_ref, v_ref, km_ref, vm_ref, nf_ref, o_ref, vx_scr, vmx_scr, *, tq, nq, hg):
    qi = pl.program_id(2)
    causal = (lax.broadcasted_iota(jnp.int32, (tq, tq), 0)
              >= lax.broadcasted_iota(jnp.int32, (tq, tq), 1))

    @pl.when(qi == 0)
    def _():
        for hh in range(hg):
            cols = slice(hh * HEAD_DIM, (hh + 1) * HEAD_DIM)
            vx_scr[hh, :, 0:HEAD_DIM] = v_ref[0, :, cols]
            vx_scr[hh, :, HEAD_DIM:] = jnp.ones((vx_scr.shape[1], HEAD_DIM), BF16)
            vmx_scr[hh, :, 0:HEAD_DIM] = vm_ref[:, cols]
            vmx_scr[hh, :, HEAD_DIM:] = jnp.ones((LANES, HEAD_DIM), BF16)

    def head(hh, c):
        cols = slice(hh * HEAD_DIM, (hh + 1) * HEAD_DIM)
        q = q_ref[0, :, cols]

        def scores(k_blk, neg_f):
            s = lax.dot_general(q, k_blk, (((1,), (1,)), ((), ())), preferred_element_type=F32)
            return s + neg_f

        lo = c * tq
        ss = [scores(km_ref[:, cols], nf_ref[hh, :, 0:LANES])]
        vs = [vmx_scr[hh]]
        if c > 0:
            ss.append(scores(k_ref[0, 0:lo, cols], nf_ref[hh, :, LANES:LANES + lo]))
            vs.append(vx_scr[hh, 0:lo, :])
        s_diag = scores(k_ref[0, lo:lo + tq, cols], nf_ref[hh, :, LANES + lo:LANES + lo + tq])
        ss.append(jnp.where(causal, s_diag, NEG_BIG))
        vs.append(vx_scr[hh, lo:lo + tq, :])
        m = functools.reduce(jnp.maximum, [jnp.max(s, axis=1, keepdims=True) for s in ss])
        ol = jnp.zeros((tq, 2 * HEAD_DIM), F32)
        for s, v_blk in zip(ss, vs):
            ol = ol + jnp.dot(jnp.exp2(s - m).astype(BF16), v_blk, preferred_element_type=F32)
        o_ref[0, :, cols] = (ol[:, :HEAD_DIM] / ol[:, HEAD_DIM:]).astype(BF16)

    for c in range(nq):
        @pl.when(qi == c)
        def _(c=c):
            for hh in range(hg):
                head(hh, c)


def _attention(z3, zm_pad, neg_f, *, tq, hg):
    b, t_len, _ = z3.shape
    nq = t_len // tq
    ng = N_HEADS // hg
    w = hg * HEAD_DIM
    blocks = (tq * w * 2 * 2 + 2 * t_len * w * 2 + 2 * LANES * w * 2
              + hg * SUBLANES * (LANES + t_len) * 4)
    resident = hg * 4 * tq * (t_len + LANES) * 4 + hg * (t_len + LANES) * 2 * HEAD_DIM * 2
    return pl.pallas_call(
        functools.partial(_attn_kernel, tq=tq, nq=nq, hg=hg),
        grid=(b, ng, nq),
        in_specs=[
            pl.BlockSpec((1, tq, w), lambda bi, gi, qi: (bi, qi, P_Q * ng + gi)),
            pl.BlockSpec((1, t_len, w), lambda bi, gi, qi: (bi, 0, P_K * ng + gi)),
            pl.BlockSpec((1, t_len, w), lambda bi, gi, qi: (bi, 0, P_V * ng + gi)),
            pl.BlockSpec((LANES, w), lambda bi, gi, qi: (0, P_K * ng + gi)),
            pl.BlockSpec((LANES, w), lambda bi, gi, qi: (0, P_V * ng + gi)),
            pl.BlockSpec((hg, 1, LANES + t_len), lambda bi, gi, qi: (bi * ng + gi, 0, 0)),
        ],
        out_specs=pl.BlockSpec((1, tq, w), lambda bi, gi, qi: (bi, qi, gi)),
        out_shape=jax.ShapeDtypeStruct((b, t_len, N_HEADS * HEAD_DIM), BF16),
        scratch_shapes=[
            pltpu.VMEM((hg, t_len, 2 * HEAD_DIM), BF16),
            pltpu.VMEM((hg, LANES, 2 * HEAD_DIM), BF16),
        ],
        compiler_params=pltpu.CompilerParams(
            dimension_semantics=("parallel", "parallel", "arbitrary"),
            vmem_limit_bytes=_vmem_limit(blocks, resident)),
        name="attention",
    )(z3, z3, z3, zm_pad, zm_pad, neg_f)


def _merge_kernel(yr_ref, ya_ref, wr_ref, wa_ref, gr_ref, ga_ref, o_ref):
    pr = jnp.dot(yr_ref[...], wr_ref[...], preferred_element_type=F32)
    pa = jnp.dot(ya_ref[...], wa_ref[...], preferred_element_type=F32)
    merged = (jax.nn.sigmoid(gr_ref[...].astype(F32)) * pr
              + jax.nn.sigmoid(ga_ref[...].astype(F32)) * pa)
    o_ref[...] = merged.astype(BF16)


def _merge(y_rnn, y_attn, w_rnn_out, w_attn_out, zg, *, tm, tn):
    m = y_rnn.shape[0]
    nn = D_MODEL // tn
    blocks = 2 * tm * D_MODEL * 2 + 2 * D_MODEL * tn * 2 + 3 * tm * tn * 2
    resident = 3 * tm * tn * 4
    return pl.pallas_call(
        _merge_kernel,
        grid=(m // tm, nn),
        in_specs=[
            pl.BlockSpec((tm, D_MODEL), lambda i, j: (i, 0)),
            pl.BlockSpec((tm, D_MODEL), lambda i, j: (i, 0)),
            pl.BlockSpec((D_MODEL, tn), lambda i, j: (0, j)),
            pl.BlockSpec((D_MODEL, tn), lambda i, j: (0, j)),
            pl.BlockSpec((tm, tn), lambda i, j: (i, P_GRNN * nn + j)),
            pl.BlockSpec((tm, tn), lambda i, j: (i, P_GATTN * nn + j)),
        ],
        out_specs=pl.BlockSpec((tm, tn), lambda i, j: (i, j)),
        out_shape=jax.ShapeDtypeStruct((m, D_MODEL), BF16),
        compiler_params=pltpu.CompilerParams(
            dimension_semantics=("parallel", "arbitrary"),
            vmem_limit_bytes=_vmem_limit(blocks, resident)),
        name="merge",
    )(y_rnn, y_attn, w_rnn_out, w_attn_out, zg, zg)


def _out_proj_kernel(x_ref, w_ref, h_ref, g_ref, o_ref):
    mix = jnp.dot(x_ref[...], w_ref[...], preferred_element_type=F32)
    o_ref[...] = h_ref[...] + _rms(mix, g_ref[...])


def _out_proj(merged, w_o, h, g_post, *, tm):
    m = merged.shape[0]
    blocks = tm * D_MODEL * (2 + 4 + 4) + D_MODEL * D_MODEL * 2
    resident = tm * D_MODEL * 4
    return pl.pallas_call(
        _out_proj_kernel,
        grid=(m // tm,),
        in_specs=[
            pl.BlockSpec((tm, D_MODEL), lambda i: (i, 0)),
            pl.BlockSpec((D_MODEL, D_MODEL), lambda i: (0, 0)),
            pl.BlockSpec((tm, D_MODEL), lambda i: (i, 0)),
            pl.BlockSpec((1, D_MODEL), lambda i: (0, 0)),
        ],
        out_specs=pl.BlockSpec((tm, D_MODEL), lambda i: (i, 0)),
        out_shape=jax.ShapeDtypeStruct((m, D_MODEL), F32),
        compiler_params=pltpu.CompilerParams(
            dimension_semantics=("parallel",),
            vmem_limit_bytes=_vmem_limit(blocks, resident)),
        name="out_proj",
    )(merged, w_o, h, g_post)


def kernel(x, meta_tokens, norm_g, ffn1_w_gu, ffn1_w_down, w_in, conv_w, conv_b, lru_w_a, lru_b_a,
           lru_w_x, lru_b_x, lru_lambda, forget_b, w_rnn_out, w_attn_out, w_o, ffn2_w_gu,
           ffn2_w_down):
    b, t_len, d = x.shape
    m = b * t_len
    g = [norm_g[0, k:k + 1] for k in range(6)]
    row = lambda v: v[0].reshape(1, -1)

    w_in_t = w_in[0].T.astype(BF16)
    w_ax = jnp.concatenate([lru_w_a[0], lru_w_x[0]], axis=-1).astype(BF16)
    w_gu1, w_d1 = ffn1_w_gu[0].astype(BF16), ffn1_w_down[0].astype(BF16)
    w_gu2, w_d2 = ffn2_w_gu[0].astype(BF16), ffn2_w_down[0].astype(BF16)
    w_r, w_a, w_out = w_rnn_out[0].astype(BF16), w_attn_out[0].astype(BF16), w_o[0].astype(BF16)
    rnn_params = (conv_w[0], row(conv_b), w_ax, row(lru_b_a), row(lru_b_x), row(lru_lambda))

    _, u_m = _ffn(meta_tokens, g[0], w_gu1, w_d1, g[1], g[2], tm=N_META)
    z_m = _in_proj(u_m, w_in_t, tm=N_META, tn=PROJ_TN)
    _, fl_m = _gate_proj(u_m, w_in_t, LANES, tm=N_META, tn=LANES)
    zeros = jnp.zeros((SUBLANES, D_MODEL), F32)
    _, h_meta = _rnn(jnp.broadcast_to(z_m[None], (SUBLANES, N_META, Z_COLS)), zeros, zeros[:1],
                     *rnn_params, tt=N_META, cb=RNN_CB)
    ctx = z_m[N_META - SUBLANES:, :D_MODEL].astype(F32)
    zm_pad = jnp.pad(z_m, ((LANES - N_META, 0), (0, 0)))

    h1, u2 = _ffn(x.reshape(m, d), g[0], w_gu1, w_d1, g[1], g[2], tm=FFN_TM)
    z = _in_proj(u2, w_in_t, tm=PROJ_TM, tn=PROJ_TN)
    zg, fl = _gate_proj(u2, w_in_t, ZG_COLS, tm=PROJ_TM, tn=PROJ_TN)
    z3 = z.reshape(b, t_len, Z_COLS)
    y_rnn, _ = _rnn(z3, ctx, h_meta[:1], *rnn_params, tt=RNN_TT, cb=RNN_CB)

    fl_meta = jnp.pad(fl_m[:, :N_HEADS].T, ((0, 0), (LANES - N_META, 0)))
    fl_real = fl[:, :N_HEADS].reshape(b, t_len, N_HEADS).transpose(0, 2, 1)
    x_all = jnp.concatenate([jnp.broadcast_to(fl_meta[None], (b, N_HEADS, LANES)), fl_real], axis=2)
    neg_f = _forget_prefix(x_all, forget_b[0].reshape(N_HEADS, 1))
    y_attn = _attention(z3, zm_pad, neg_f.reshape(b * N_HEADS, 1, LANES + t_len),
                        tq=ATTN_TQ, hg=ATTN_HEADS_PER_STEP)

    merged = _merge(y_rnn.reshape(m, d), y_attn.reshape(m, d), w_r, w_a, zg, tm=MERGE_TM, tn=MERGE_TN)
    h2 = _out_proj(merged, w_out, h1, g[3], tm=OUT_TM)
    (h3,) = _ffn(h2, g[4], w_gu2, w_d2, g[5], None, tm=FFN_TM)
    return h3.reshape(b, t_len, d)
```

```python
import functools

import jax
import jax.numpy as jnp
from jax import lax
from jax.experimental import pallas as pl
from jax.experimental.pallas import tpu as pltpu

D_MODEL = 2048
N_META = 16
D_FF = 5632
N_HEADS = 16
HEAD_DIM = 128
LRU_BS = 128
LRU_BLOCKS = D_MODEL // LRU_BS
CONV_W = 4
LRU_C = 8.0
EPS = 1e-6
(P_XR, P_GR, P_Q, P_K, P_V) = range(5)
Z_COLS = 5 * D_MODEL
(P_GRNN, P_GATTN) = range(2)
ZG_COLS = 2 * D_MODEL
FFN_TM, FFN_TF = 512, 512
FFN_SPLIT_TM = 1024
PROJ_TM, PROJ_TN = 1024, 2048
RNN_TT, RNN_CB = 256, 512
ATTN_TQ, ATTN_HEADS_PER_STEP = 512, 4
MERGE_TM, MERGE_TN = 1024, 512
OUT_TM = 512

LANES = 128
SUBLANES = 8
BF16_ROWS = 2 * SUBLANES
VMEM_BYTES_V7X = 64 * 1024 * 1024
NEG_BIG = -1e30
LOG2E = 1.4426950408889634
QK_SCALE_LOG2 = HEAD_DIM ** -0.5 * LOG2E

F32 = jnp.float32
BF16 = jnp.bfloat16


def _vmem_limit(pipelined_bytes, resident_bytes):
    return min(2 * pipelined_bytes + resident_bytes + (4 << 20), VMEM_BYTES_V7X - (4 << 20))


def _rms(x, g):
    return (x * lax.rsqrt(jnp.mean(x * x, axis=-1, keepdims=True) + EPS)) * g


def _ffn_kernel(h_ref, gpre_ref, wg_ref, wu_ref, wd_ref, gpost_ref, *rest, with_next):
    if with_next:
        gnext_ref, out_ref, unext_ref, u_scr = rest
    else:
        out_ref, u_scr = rest
    j = pl.program_id(1)

    @pl.when(j == 0)
    def _():
        u_scr[...] = _rms(h_ref[...], gpre_ref[...]).astype(BF16)
        out_ref[...] = jnp.zeros_like(out_ref)

    u = u_scr[...]
    gate = jnp.dot(u, wg_ref[...], preferred_element_type=F32)
    up = jnp.dot(u, wu_ref[...], preferred_element_type=F32)
    act = ((gate * jax.nn.sigmoid(gate)) * up).astype(BF16)
    out_ref[...] += jnp.dot(act, wd_ref[...], preferred_element_type=F32)

    @pl.when(j == pl.num_programs(1) - 1)
    def _():
        h_new = h_ref[...] + _rms(out_ref[...], 0.5 * gpost_ref[...])
        out_ref[...] = h_new
        if with_next:
            unext_ref[...] = _rms(h_new, gnext_ref[...]).astype(BF16)


def _ffn(h, g_pre, w_gu, w_down, g_post, g_next, *, tm):
    m = h.shape[0]
    tf = FFN_TF
    nf = D_FF // tf
    with_next = g_next is not None
    row_tile = pl.BlockSpec((tm, D_MODEL), lambda i, j: (i, 0))
    gain = pl.BlockSpec((1, D_MODEL), lambda i, j: (0, 0))
    blocks = tm * D_MODEL * (4 + 4 + 2 * with_next) + 3 * D_MODEL * tf * 2
    resident = tm * D_MODEL * (2 + 4) + tm * tf * (4 + 4 + 2)
    return pl.pallas_call(
        functools.partial(_ffn_kernel, with_next=with_next),
        grid=(m // tm, nf),
        in_specs=[
            row_tile,
            gain,
            pl.BlockSpec((D_MODEL, tf), lambda i, j: (0, j)),
            pl.BlockSpec((D_MODEL, tf), lambda i, j: (0, j + nf)),
            pl.BlockSpec((tf, D_MODEL), lambda i, j: (j, 0)),
            gain,
        ] + [gain] * with_next,
        out_specs=[row_tile] * (1 + with_next),
        out_shape=[jax.ShapeDtypeStruct((m, D_MODEL), F32)]
        + [jax.ShapeDtypeStruct((m, D_MODEL), BF16)] * with_next,
        scratch_shapes=[pltpu.VMEM((tm, D_MODEL), BF16)],
        compiler_params=pltpu.CompilerParams(
            dimension_semantics=("parallel", "arbitrary"),
            vmem_limit_bytes=_vmem_limit(blocks, resident)),
        name="ffn",
    )(h, g_pre, w_gu, w_gu, w_down, g_post, *([g_next] * with_next))


def _ffn_up_kernel(h_ref, gpre_ref, wg_ref, wu_ref, act_ref, u_scr):
    @pl.when(pl.program_id(1) == 0)
    def _():
        u_scr[...] = _rms(h_ref[...], gpre_ref[...]).astype(BF16)

    u = u_scr[...]
    gate = jnp.dot(u, wg_ref[...], preferred_element_type=F32)
    up = jnp.dot(u, wu_ref[...], preferred_element_type=F32)
    act_ref[...] = ((gate * jax.nn.sigmoid(gate)) * up).astype(BF16)


def _ffn_down_kernel(act_ref, wd_ref, h_ref, gpost_ref, out_ref):
    j = pl.program_id(1)

    @pl.when(j == 0)
    def _():
        out_ref[...] = jnp.zeros_like(out_ref)

    out_ref[...] += jnp.dot(act_ref[...], wd_ref[...], preferred_element_type=F32)

    @pl.when(j == pl.num_programs(1) - 1)
    def _():
        out_ref[...] = h_ref[...] + _rms(out_ref[...], 0.5 * gpost_ref[...])


def _ffn_split(h, g_pre, w_gu, w_down, g_post, *, tm):
    m = h.shape[0]
    tf = FFN_TF
    nf = D_FF // tf
    row_tile = pl.BlockSpec((tm, D_MODEL), lambda i, j: (i, 0))
    gain = pl.BlockSpec((1, D_MODEL), lambda i, j: (0, 0))
    act = pl.pallas_call(
        _ffn_up_kernel,
        grid=(m // tm, nf),
        in_specs=[
            row_tile, gain,
            pl.BlockSpec((D_MODEL, tf), lambda i, j: (0, j)),
            pl.BlockSpec((D_MODEL, tf), lambda i, j: (0, j + nf)),
        ],
        out_specs=pl.BlockSpec((tm, tf), lambda i, j: (i, j)),
        out_shape=jax.ShapeDtypeStruct((m, D_FF), BF16),
        scratch_shapes=[pltpu.VMEM((tm, D_MODEL), BF16)],
        compiler_params=pltpu.CompilerParams(
            dimension_semantics=("parallel", "arbitrary"),
            vmem_limit_bytes=_vmem_limit(tm * D_MODEL * 4 + 2 * D_MODEL * tf * 2 + tm * tf * 2,
                                         tm * D_MODEL * 2 + tm * tf * (4 + 4 + 2))),
        name="ffn_up",
    )(h, g_pre, w_gu, w_gu)
    return pl.pallas_call(
        _ffn_down_kernel,
        grid=(m // tm, nf),
        in_specs=[
            pl.BlockSpec((tm, tf), lambda i, j: (i, j)),
            pl.BlockSpec((tf, D_MODEL), lambda i, j: (j, 0)),
            row_tile, gain,
        ],
        out_specs=row_tile,
        out_shape=jax.ShapeDtypeStruct((m, D_MODEL), F32),
        compiler_params=pltpu.CompilerParams(
            dimension_semantics=("parallel", "arbitrary"),
            vmem_limit_bytes=_vmem_limit(tm * tf * 2 + tf * D_MODEL * 2 + 2 * tm * D_MODEL * 4,
                                         tm * D_MODEL * 4)),
        name="ffn_down",
    )(act, w_down, h, g_post)


def _dot_nt(x, wt):
    return lax.dot_general(x, wt, (((1,), (1,)), ((), ())), preferred_element_type=F32)


def _in_proj_kernel(u_ref, wt_ref, z_ref, *, tn):
    j = pl.program_id(0)
    is_q = (j >= P_Q * D_MODEL // tn) & (j < (P_Q + 1) * D_MODEL // tn)
    factor = jnp.where(is_q, QK_SCALE_LOG2, 1.0)
    z_ref[...] = (_dot_nt(u_ref[...], wt_ref[...]) * factor).astype(BF16)


def _in_proj(u, w_in_t, *, tm, tn):
    m = u.shape[0]
    blocks = tm * D_MODEL * 2 + D_MODEL * tn * 2 + tm * tn * 2
    resident = tm * tn * 4
    return pl.pallas_call(
        functools.partial(_in_proj_kernel, tn=tn),
        grid=(Z_COLS // tn, m // tm),
        in_specs=[
            pl.BlockSpec((tm, D_MODEL), lambda j, i: (i, 0)),
            pl.BlockSpec((tn, D_MODEL), lambda j, i: (j, 0)),
        ],
        out_specs=pl.BlockSpec((tm, tn), lambda j, i: (i, j)),
        out_shape=jax.ShapeDtypeStruct((m, Z_COLS), BF16),
        compiler_params=pltpu.CompilerParams(
            dimension_semantics=("parallel", "parallel"),
            vmem_limit_bytes=_vmem_limit(blocks, resident)),
        name="in_proj",
    )(u, w_in_t)


def _gate_proj_kernel(u_ref, wt_ref, wflt_ref, zg_ref, fl_ref):
    u = u_ref[...]
    zg_ref[...] = _dot_nt(u, wt_ref[...]).astype(BF16)

    @pl.when(pl.program_id(1) == 0)
    def _():
        fl_ref[...] = _dot_nt(u, wflt_ref[...])


def _gate_proj(u, w_in_t, n_cols, *, tm, tn):
    m = u.shape[0]
    blocks = tm * D_MODEL * 2 + D_MODEL * tn * 2 + D_MODEL * LANES * 2 + tm * tn * 2 + tm * LANES * 4
    resident = tm * tn * 4
    return pl.pallas_call(
        _gate_proj_kernel,
        grid=(m // tm, n_cols // tn),
        in_specs=[
            pl.BlockSpec((tm, D_MODEL), lambda i, j: (i, 0)),
            pl.BlockSpec((pl.Element(tn), pl.Element(D_MODEL)),
                         lambda i, j: (pl.multiple_of(Z_COLS + N_HEADS + j * tn, BF16_ROWS), 0)),
            pl.BlockSpec((pl.Element(LANES), pl.Element(D_MODEL)), lambda i, j: (Z_COLS, 0)),
        ],
        out_specs=[
            pl.BlockSpec((tm, tn), lambda i, j: (i, j)),
            pl.BlockSpec((tm, LANES), lambda i, j: (i, 0)),
        ],
        out_shape=[
            jax.ShapeDtypeStruct((m, n_cols), BF16),
            jax.ShapeDtypeStruct((m, LANES), F32),
        ],
        compiler_params=pltpu.CompilerParams(
            dimension_semantics=("parallel", "arbitrary"),
            vmem_limit_bytes=_vmem_limit(blocks, resident)),
        name="gate_proj",
    )(u, w_in_t, w_in_t)


def _gelu_tanh(x):
    c = (2.0 / jnp.pi) ** 0.5
    half = 0.5 * x
    return half + half * jnp.tanh(x * (c + (c * 0.044715) * (x * x)))


def _rnn_kernel(xr_ref, gr_ref, ctx_ref, h0_ref, cw_ref, cb_ref, wax_ref, ba_ref, bx_ref, lam_ref,
                y_ref, hlast_ref, xbuf, a_scr, b_scr, h_scr, carry_scr, *, tt, cb):
    t = pl.program_id(1)
    nslab = cb // LRU_BS

    @pl.when(t == 0)
    def _():
        xbuf[:, 0:SUBLANES, :] = jnp.broadcast_to(ctx_ref[...][None], (SUBLANES, SUBLANES, cb))
        carry_scr[...] = jnp.broadcast_to(h0_ref[...], (SUBLANES, cb))

    xb = xr_ref[...]
    x = xb.astype(F32)
    xbuf[:, SUBLANES:, :] = x[:, 0:SUBLANES, :]
    shift = (lax.broadcasted_iota(jnp.int32, (tt, tt), 0)
             - lax.broadcasted_iota(jnp.int32, (tt, tt), 1))
    select = jnp.concatenate([(shift == k).astype(BF16) for k in (3, 2, 1)], axis=0)
    heads = [xbuf[:, pl.ds(SUBLANES - k, SUBLANES), :] for k in (3, 2, 1)]
    shifted = [[], [], []]
    for b in range(SUBLANES):
        body = jnp.dot(select, xb[b], preferred_element_type=F32)
        for s in range(3):
            shifted[s].append(jnp.concatenate(
                [heads[s][b], body[s * tt + SUBLANES:(s + 1) * tt]], axis=0))
    xc = jnp.stack(shifted[0]) * cw_ref[0:1, :]
    xc = xc + jnp.stack(shifted[1]) * cw_ref[1:2, :]
    xc = xc + jnp.stack(shifted[2]) * cw_ref[2:3, :]
    xc = xc + x * cw_ref[3:4, :]
    xc = xc + cb_ref[...]
    xbuf[:, 0:SUBLANES, :] = x[:, tt - SUBLANES:, :]

    for n in range(nslab):
        cols = slice(n * LRU_BS, (n + 1) * LRU_BS)
        xn = xc[:, :, cols].reshape(SUBLANES * tt, LRU_BS)
        g = jnp.dot(xn.astype(BF16), wax_ref[n], preferred_element_type=F32)
        r = jax.nn.sigmoid(g[:, :LRU_BS] + ba_ref[:, cols])
        i = jax.nn.sigmoid(g[:, LRU_BS:] + bx_ref[:, cols])
        neg_lam = -lam_ref[:, cols]
        softplus = jnp.maximum(neg_lam, 0.0) + jnp.log1p(jnp.exp(-jnp.abs(neg_lam)))
        a = jnp.exp2((-LRU_C * LOG2E * softplus) * r)
        v = 1.0 - a * a
        bv = jnp.where(v > 0.0, v * lax.rsqrt(v), 0.0) * (i * xn)
        for b in range(SUBLANES):
            a_scr[n, pl.ds(b, tt, stride=SUBLANES), :] = a[b * tt:(b + 1) * tt]
            b_scr[n, pl.ds(b, tt, stride=SUBLANES), :] = bv[b * tt:(b + 1) * tt]

    def step(tl, hs):
        rows = pl.ds(pl.multiple_of(tl * SUBLANES, SUBLANES), SUBLANES)
        hs = tuple(a_scr[n, rows, :] * hs[n] + b_scr[n, rows, :] for n in range(nslab))
        for n in range(nslab):
            h_scr[n, rows, :] = hs[n]
        return hs

    hs = tuple(carry_scr[:, n * LRU_BS:(n + 1) * LRU_BS] for n in range(nslab))
    hs = lax.fori_loop(0, tt, step, hs, unroll=8)
    for n in range(nslab):
        cols = slice(n * LRU_BS, (n + 1) * LRU_BS)
        carry_scr[:, cols] = hs[n]
        hlast_ref[:, cols] = hs[n]
        for b in range(SUBLANES):
            gate = _gelu_tanh(gr_ref[b, :, cols].astype(F32))
            y_ref[b, :, cols] = (h_scr[n, pl.ds(b, tt, stride=SUBLANES), :] * gate).astype(BF16)


def _rnn(z3, ctx, h0, conv_w, conv_b, w_ax, b_a, b_x, lam, *, tt, cb):
    b, t_len, _ = z3.shape
    assert b == SUBLANES
    ncb = D_MODEL // cb
    nblk = cb // LRU_BS
    vec = lambda rows: pl.BlockSpec((rows, cb), lambda ci, ti: (0, ci))
    blocks = 3 * b * tt * cb * 2 + nblk * LRU_BS * 2 * LRU_BS * 2 + b * cb * 4
    resident = b * 2 * SUBLANES * cb * 4 + 3 * b * tt * cb * 4 + 6 * b * tt * cb * 4
    return pl.pallas_call(
        functools.partial(_rnn_kernel, tt=tt, cb=cb),
        grid=(ncb, t_len // tt),
        in_specs=[
            pl.BlockSpec((b, tt, cb), lambda ci, ti: (0, ti, P_XR * ncb + ci)),
            pl.BlockSpec((b, tt, cb), lambda ci, ti: (0, ti, P_GR * ncb + ci)),
            vec(SUBLANES), vec(1), vec(CONV_W), vec(1),
            pl.BlockSpec((nblk, LRU_BS, 2 * LRU_BS), lambda ci, ti: (ci, 0, 0)),
            vec(1), vec(1), vec(1),
        ],
        out_specs=[
            pl.BlockSpec((b, tt, cb), lambda ci, ti: (0, ti, ci)),
            pl.BlockSpec((b, cb), lambda ci, ti: (0, ci)),
        ],
        out_shape=[
            jax.ShapeDtypeStruct((b, t_len, D_MODEL), BF16),
            jax.ShapeDtypeStruct((b, D_MODEL), F32),
        ],
        scratch_shapes=[
            pltpu.VMEM((b, 2 * SUBLANES, cb), F32),
            pltpu.VMEM((nblk, b * tt, LRU_BS), F32),
            pltpu.VMEM((nblk, b * tt, LRU_BS), F32),
            pltpu.VMEM((nblk, b * tt, LRU_BS), F32),
            pltpu.VMEM((b, cb), F32),
        ],
        compiler_params=pltpu.CompilerParams(
            dimension_semantics=("parallel", "arbitrary"),
            vmem_limit_bytes=_vmem_limit(blocks, resident)),
        name="rnn",
    )(z3, z3, ctx, h0, conv_w, conv_b, w_ax, b_a, b_x, lam)


def _forget_kernel(x_ref, fb_ref, nf_ref, *, n_blocks, first_valid):
    lane = lax.broadcasted_iota(jnp.int32, (N_HEADS, LANES), 1)
    fb = fb_ref[...]
    run = jnp.zeros((N_HEADS, 1), F32)
    for k in range(n_blocks):
        x = x_ref[0, :, k * LANES:(k + 1) * LANES] + fb
        lf = jnp.minimum(x, 0.0) - jnp.log1p(jnp.exp(-jnp.abs(x)))
        if k == 0:
            lf = jnp.where(lane >= first_valid, lf, 0.0)
        d = 1
        while d < LANES:
            lf = jnp.where(lane >= d, lf + pltpu.roll(lf, d, axis=1), lf)
            d *= 2
        f = lf + run
        run = f[:, LANES - 1:LANES]
        neg = f * -LOG2E
        if k == 0:
            neg = jnp.where(lane >= first_valid, neg, NEG_BIG)
        nf_ref[0, :, k * LANES:(k + 1) * LANES] = neg


def _forget_prefix(x_all, forget_b):
    b, _, n = x_all.shape
    return pl.pallas_call(
        functools.partial(_forget_kernel, n_blocks=n // LANES, first_valid=LANES - N_META),
        grid=(b,),
        in_specs=[
            pl.BlockSpec((1, N_HEADS, n), lambda bi: (bi, 0, 0)),
            pl.BlockSpec((N_HEADS, 1), lambda bi: (0, 0)),
        ],
        out_specs=pl.BlockSpec((1, N_HEADS, n), lambda bi: (bi, 0, 0)),
        out_shape=jax.ShapeDtypeStruct((b, N_HEADS, n), F32),
        compiler_params=pltpu.CompilerParams(dimension_semantics=("parallel",)),
        name="forget_prefix",
    )(x_all, forget_b)


def _attn_kernel(q_ref, k_ref, v_ref, km_ref, vm_ref, nf_ref, o_ref, vx_scr, vmx_scr, *, tq, nq, hg):
    qi = pl.program_id(2)
    causal = (lax.broadcasted_iota(jnp.int32, (tq, tq), 0)
              >= lax.broadcasted_iota(jnp.int32, (tq, tq), 1))

    @pl.when(qi == 0)
    def _():
        for hh in range(hg):
            cols = slice(hh * HEAD_DIM, (hh + 1) * HEAD_DIM)
            vx_scr[hh, :, 0:HEAD_DIM] = v_ref[0, :, cols]
            vx_scr[hh, :, HEAD_DIM:] = jnp.ones((vx_scr.shape[1], HEAD_DIM), BF16)
            vmx_scr[hh, :, 0:HEAD_DIM] = vm_ref[:, cols]
            vmx_scr[hh, :, HEAD_DIM:] = jnp.ones((LANES, HEAD_DIM), BF16)

    def head(hh, c):
        cols = slice(hh * HEAD_DIM, (hh + 1) * HEAD_DIM)
        q = q_ref[0, :, cols]

        def scores(k_blk, neg_f):
            s = lax.dot_general(q, k_blk, (((1,), (1,)), ((), ())), preferred_element_type=F32)
            return s + neg_f

        lo = c * tq
        ss = [scores(km_ref[:, cols], nf_ref[hh, :, 0:LANES])]
        vs = [vmx_scr[hh]]
        if c > 0:
            ss.append(scores(k_ref[0, 0:lo, cols], nf_ref[hh, :, LANES:LANES + lo]))
            vs.append(vx_scr[hh, 0:lo, :])
        s_diag = scores(k_ref[0, lo:lo + tq, cols], nf_ref[hh, :, LANES + lo:LANES + lo + tq])
        ss.append(jnp.where(causal, s_diag, NEG_BIG))
        vs.append(vx_scr[hh, lo:lo + tq, :])
        m = functools.reduce(jnp.maximum, [jnp.max(s, axis=1, keepdims=True) for s in ss])
        ol = jnp.zeros((tq, 2 * HEAD_DIM), F32)
        for s, v_blk in zip(ss, vs):
            ol = ol + jnp.dot(jnp.exp2(s - m).astype(BF16), v_blk, preferred_element_type=F32)
        o_ref[0, :, cols] = (ol[:, :HEAD_DIM] / ol[:, HEAD_DIM:]).astype(BF16)

    for c in range(nq):
        @pl.when(qi == c)
        def _(c=c):
            for hh in range(hg):
                head(hh, c)


def _attention(z3, zm_pad, neg_f, *, tq, hg):
    b, t_len, _ = z3.shape
    nq = t_len // tq
    ng = N_HEADS // hg
    w = hg * HEAD_DIM
    blocks = (tq * w * 2 * 2 + 2 * t_len * w * 2 + 2 * LANES * w * 2
              + hg * SUBLANES * (LANES + t_len) * 4)
    resident = hg * 4 * tq * (t_len + LANES) * 4 + hg * (t_len + LANES) * 2 * HEAD_DIM * 2
    return pl.pallas_call(
        functools.partial(_attn_kernel, tq=tq, nq=nq, hg=hg),
        grid=(b, ng, nq),
        in_specs=[
            pl.BlockSpec((1, tq, w), lambda bi, gi, qi: (bi, qi, P_Q * ng + gi)),
            pl.BlockSpec((1, t_len, w), lambda bi, gi, qi: (bi, 0, P_K * ng + gi)),
            pl.BlockSpec((1, t_len, w), lambda bi, gi, qi: (bi, 0, P_V * ng + gi)),
            pl.BlockSpec((LANES, w), lambda bi, gi, qi: (0, P_K * ng + gi)),
            pl.BlockSpec((LANES, w), lambda bi, gi, qi: (0, P_V * ng + gi)),
            pl.BlockSpec((hg, 1, LANES + t_len), lambda bi, gi, qi: (bi * ng + gi, 0, 0)),
        ],
        out_specs=pl.BlockSpec((1, tq, w), lambda bi, gi, qi: (bi, qi, gi)),
        out_shape=jax.ShapeDtypeStruct((b, t_len, N_HEADS * HEAD_DIM), BF16),
        scratch_shapes=[
            pltpu.VMEM((hg, t_len, 2 * HEAD_DIM), BF16),
            pltpu.VMEM((hg, LANES, 2 * HEAD_DIM), BF16),
        ],
        compiler_params=pltpu.CompilerParams(
            dimension_semantics=("parallel", "parallel", "arbitrary"),
            vmem_limit_bytes=_vmem_limit(blocks, resident)),
        name="attention",
    )(z3, z3, z3, zm_pad, zm_pad, neg_f)


def _merge_kernel(yr_ref, ya_ref, wr_ref, wa_ref, gr_ref, ga_ref, o_ref):
    pr = jnp.dot(yr_ref[...], wr_ref[...], preferred_element_type=F32)
    pa = jnp.dot(ya_ref[...], wa_ref[...], preferred_element_type=F32)
    merged = (jax.nn.sigmoid(gr_ref[...].astype(F32)) * pr
              + jax.nn.sigmoid(ga_ref[...].astype(F32)) * pa)
    o_ref[...] = merged.astype(BF16)


def _merge(y_rnn, y_attn, w_rnn_out, w_attn_out, zg, *, tm, tn):
    m = y_rnn.shape[0]
    nn = D_MODEL // tn
    blocks = 2 * tm * D_MODEL * 2 + 2 * D_MODEL * tn * 2 + 3 * tm * tn * 2
    resident = 3 * tm * tn * 4
    return pl.pallas_call(
        _merge_kernel,
        grid=(m // tm, nn),
        in_specs=[
            pl.BlockSpec((tm, D_MODEL), lambda i, j: (i, 0)),
            pl.BlockSpec((tm, D_MODEL), lambda i, j: (i, 0)),
            pl.BlockSpec((D_MODEL, tn), lambda i, j: (0, j)),
            pl.BlockSpec((D_MODEL, tn), lambda i, j: (0, j)),
            pl.BlockSpec((tm, tn), lambda i, j: (i, P_GRNN * nn + j)),
            pl.BlockSpec((tm, tn), lambda i, j: (i, P_GATTN * nn + j)),
        ],
        out_specs=pl.BlockSpec((tm, tn), lambda i, j: (i, j)),
        out_shape=jax.ShapeDtypeStruct((m, D_MODEL), BF16),
        compiler_params=pltpu.CompilerParams(
            dimension_semantics=("parallel", "arbitrary"),
            vmem_limit_bytes=_vmem_limit(blocks, resident)),
        name="merge",
    )(y_rnn, y_attn, w_rnn_out, w_attn_out, zg, zg)


def _out_proj_kernel(x_ref, w_ref, h_ref, g_ref, o_ref):
    mix = jnp.dot(x_ref[...], w_ref[...], preferred_element_type=F32)
    o_ref[...] = h_ref[...] + _rms(mix, g_ref[...])


def _out_proj(merged, w_o, h, g_post, *, tm):
    m = merged.shape[0]
    blocks = tm * D_MODEL * (2 + 4 + 4) + D_MODEL * D_MODEL * 2
    resident = tm * D_MODEL * 4
    return pl.pallas_call(
        _out_proj_kernel,
        grid=(m // tm,),
        in_specs=[
            pl.BlockSpec((tm, D_MODEL), lambda i: (i, 0)),
            pl.BlockSpec((D_MODEL, D_MODEL), lambda i: (0, 0)),
            pl.BlockSpec((tm, D_MODEL), lambda i: (i, 0)),
            pl.BlockSpec((1, D_MODEL), lambda i: (0, 0)),
        ],
        out_specs=pl.BlockSpec((tm, D_MODEL), lambda i: (i, 0)),
        out_shape=jax.ShapeDtypeStruct((m, D_MODEL), F32),
        compiler_params=pltpu.CompilerParams(
            dimension_semantics=("parallel",),
            vmem_limit_bytes=_vmem_limit(blocks, resident)),
        name="out_proj",
    )(merged, w_o, h, g_post)


def kernel(x, meta_tokens, norm_g, ffn1_w_gu, ffn1_w_down, w_in, conv_w, conv_b, lru_w_a, lru_b_a,
           lru_w_x, lru_b_x, lru_lambda, forget_b, w_rnn_out, w_attn_out, w_o, ffn2_w_gu,
           ffn2_w_down):
    b, t_len, d = x.shape
    m = b * t_len
    g = [norm_g[0, k:k + 1] for k in range(6)]
    row = lambda v: v[0].reshape(1, -1)

    w_in_t = w_in[0].T.astype(BF16)
    w_ax = jnp.concatenate([lru_w_a[0], lru_w_x[0]], axis=-1).astype(BF16)
    w_gu1, w_d1 = ffn1_w_gu[0].astype(BF16), ffn1_w_down[0].astype(BF16)
    w_gu2, w_d2 = ffn2_w_gu[0].astype(BF16), ffn2_w_down[0].astype(BF16)
    w_r, w_a, w_out = w_rnn_out[0].astype(BF16), w_attn_out[0].astype(BF16), w_o[0].astype(BF16)
    rnn_params = (conv_w[0], row(conv_b), w_ax, row(lru_b_a), row(lru_b_x), row(lru_lambda))

    _, u_m = _ffn(meta_tokens, g[0], w_gu1, w_d1, g[1], g[2], tm=N_META)
    z_m = _in_proj(u_m, w_in_t, tm=N_META, tn=PROJ_TN)
    _, fl_m = _gate_proj(u_m, w_in_t, LANES, tm=N_META, tn=LANES)
    zeros = jnp.zeros((SUBLANES, D_MODEL), F32)
    _, h_meta = _rnn(jnp.broadcast_to(z_m[None], (SUBLANES, N_META, Z_COLS)), zeros, zeros[:1],
                     *rnn_params, tt=N_META, cb=RNN_CB)
    ctx = z_m[N_META - SUBLANES:, :D_MODEL].astype(F32)
    zm_pad = jnp.pad(z_m, ((LANES - N_META, 0), (0, 0)))

    h1, u2 = _ffn(x.reshape(m, d), g[0], w_gu1, w_d1, g[1], g[2], tm=FFN_TM)
    z = _in_proj(u2, w_in_t, tm=PROJ_TM, tn=PROJ_TN)
    zg, fl = _gate_proj(u2, w_in_t, ZG_COLS, tm=PROJ_TM, tn=PROJ_TN)
    z3 = z.reshape(b, t_len, Z_COLS)
    y_rnn, _ = _rnn(z3, ctx, h_meta[:1], *rnn_params, tt=RNN_TT, cb=RNN_CB)

    fl_meta = jnp.pad(fl_m[:, :N_HEADS].T, ((0, 0), (LANES - N_META, 0)))
    fl_real = fl[:, :N_HEADS].reshape(b, t_len, N_HEADS).transpose(0, 2, 1)
    x_all = jnp.concatenate([jnp.broadcast_to(fl_meta[None], (b, N_HEADS, LANES)), fl_real], axis=2)
    neg_f = _forget_prefix(x_all, forget_b[0].reshape(N_HEADS, 1))
    y_attn = _attention(z3, zm_pad, neg_f.reshape(b * N_HEADS, 1, LANES + t_len),
                        tq=ATTN_TQ, hg=ATTN_HEADS_PER_STEP)

    merged = _merge(y_rnn.reshape(m, d), y_attn.reshape(m, d), w_r, w_a, zg, tm=MERGE_TM, tn=MERGE_TN)
    h2 = _out_proj(merged, w_out, h1, g[3], tm=OUT_TM)
    h3 = _ffn_split(h2, g[4], w_gu2, w_d2, g[5], tm=FFN_SPLIT_TM)
    return h3.reshape(b, t_len, d)
```
